```python
import jax, jax.numpy as jnp
from jax import lax
import numpy as np


D_MODEL = 2048
BATCH = 16
SEQ = 256
DEPTH = 4
DEC_BATCH = 8
DEC_SEQ = 4096
PAST_LEN = 512

GRID_W = 64
BLOCK = 128
WINDOW = 128
N_HEADS = 16
N_KV_HEADS = 4
GROUP = N_HEADS // N_KV_HEADS
HEAD_DIM = 64
ATTN_WIDTH = N_HEADS * HEAD_DIM
KV_WIDTH = N_KV_HEADS * HEAD_DIM
ATTN_SCALE = HEAD_DIM ** -0.5
ROPE_BASE = 10000.0
SGU_HEADS = 4
SGU_HEAD_DIM = 128
SGU_WIDTH = SGU_HEADS * SGU_HEAD_DIM
CHUNK = 128
CONV_WIDTH = 512
CONV_K = 3
D_FF = 5632
MIX_WIDTH = ATTN_WIDTH + SGU_WIDTH + CONV_WIDTH
IN_WIDTH = ATTN_WIDTH + 2 * KV_WIDTH + 2 * SGU_WIDTH + 3 * CONV_WIDTH
IN_SPLITS = (ATTN_WIDTH,
             ATTN_WIDTH + KV_WIDTH,
             ATTN_WIDTH + 2 * KV_WIDTH,
             ATTN_WIDTH + 2 * KV_WIDTH + SGU_WIDTH,
             ATTN_WIDTH + 2 * KV_WIDTH + 2 * SGU_WIDTH,
             ATTN_WIDTH + 2 * KV_WIDTH + 2 * SGU_WIDTH + CONV_WIDTH,
             ATTN_WIDTH + 2 * KV_WIDTH + 2 * SGU_WIDTH + 2 * CONV_WIDTH)
N_MOD = 9
DEEPNORM_ALPHA = (2 * DEPTH) ** 0.25
DEEPNORM_BETA = (8 * DEPTH) ** -0.25
LN_EPS = 1e-5
NEG_INF = -1e30

kernel_name = "hybrid_diffusion_parallel_heads_step"


def _layer_norm(x, g, b):
    xf = x.astype(jnp.float32)
    mu = xf.mean(-1, keepdims=True)
    var = jnp.square(xf - mu).mean(-1, keepdims=True)
    return ((xf - mu) * lax.rsqrt(var + LN_EPS)).astype(x.dtype) * g + b


def _norm_gain(x, g):
    xf = x.astype(jnp.float32)
    mu = xf.mean(-1, keepdims=True)
    var = jnp.square(xf - mu).mean(-1, keepdims=True)
    return ((xf - mu) * lax.rsqrt(var + LN_EPS)).astype(x.dtype) * g


def _modulation(cvec, w, b):
    m = jax.nn.silu(cvec) @ w + b
    m = m.reshape(cvec.shape[:-1] + (N_MOD, 1, D_MODEL))
    return [m[..., i, :, :] for i in range(N_MOD)]


def _swiglu(h, wi, wo):
    g, u = jnp.split(h @ wi, 2, axis=-1)
    return (jax.nn.silu(g) * u) @ wo


def _axial_rope(rows, dtype):
    row = jnp.repeat(jnp.arange(rows, dtype=jnp.float32), GRID_W)
    col = jnp.tile(jnp.arange(GRID_W, dtype=jnp.float32), rows)
    n_freq = HEAD_DIM // 4
    inv = ROPE_BASE ** (-jnp.arange(n_freq, dtype=jnp.float32) / n_freq)
    ang = jnp.concatenate([row[:, None] * inv, col[:, None] * inv], axis=-1)
    return jnp.cos(ang).astype(dtype), jnp.sin(ang).astype(dtype)


def _apply_rope(x, cos, sin):
    xr = x.reshape(x.shape[:-1] + (HEAD_DIM // 2, 2))
    x0, x1 = xr[..., 0], xr[..., 1]
    c = cos[None, :, None, :]
    s = sin[None, :, None, :]
    out = jnp.stack([x0 * c - x1 * s, x0 * s + x1 * c], axis=-1)
    return out.reshape(x.shape).astype(x.dtype)


def _sink_softmax(s, sink):
    sk = jnp.broadcast_to(sink.reshape(N_KV_HEADS, GROUP, 1, 1).astype(jnp.float32), s.shape[:-1] + (1,))
    p = jax.nn.softmax(jnp.concatenate([s, sk], axis=-1), axis=-1)
    return p[..., :-1]


def _context_attention(q, k, v, sink):
    B, L = q.shape[:2]
    nb = L // BLOCK
    qb = q.reshape(B, nb, BLOCK, N_KV_HEADS, GROUP, HEAD_DIM).transpose(1, 0, 2, 3, 4, 5)

    def one(qblk):
        s = jnp.einsum('bqkgd,bskd->bkgqs', qblk, k).astype(jnp.float32) * ATTN_SCALE
        p = _sink_softmax(s, sink).astype(v.dtype)
        return jnp.einsum('bkgqs,bskd->bqkgd', p, v)

    out = lax.map(one, qb)
    return out.transpose(1, 0, 2, 3, 4, 5).reshape(B, L, ATTN_WIDTH)


def _latent_attention(q, k, v, kc, vc, sink):
    B, L = q.shape[:2]
    nb = L // BLOCK
    kp = jnp.pad(k, ((0, 0), (BLOCK, BLOCK), (0, 0), (0, 0)))
    vp = jnp.pad(v, ((0, 0), (BLOCK, BLOCK), (0, 0), (0, 0)))
    qb = q.reshape(B, nb, BLOCK, N_KV_HEADS, GROUP, HEAD_DIM).transpose(1, 0, 2, 3, 4, 5)
    q_rel = jnp.arange(BLOCK)
    k_rel = jnp.arange(3 * BLOCK) - BLOCK
    band = jnp.abs(q_rel[:, None] - k_rel[None, :]) <= WINDOW

    def one(args):
        i, qblk = args
        kblk = lax.dynamic_slice_in_dim(kp, i * BLOCK, 3 * BLOCK, axis=1)
        vblk = lax.dynamic_slice_in_dim(vp, i * BLOCK, 3 * BLOCK, axis=1)
        kpos = i * BLOCK + k_rel
        valid = band & ((kpos >= 0) & (kpos < L))[None, :]
        s_loc = jnp.einsum('bqkgd,bskd->bkgqs', qblk, kblk).astype(jnp.float32) * ATTN_SCALE
        s_loc = jnp.where(valid, s_loc, NEG_INF)
        s_ctx = jnp.einsum('bqkgd,bskd->bkgqs', qblk, kc).astype(jnp.float32) * ATTN_SCALE
        p = _sink_softmax(jnp.concatenate([s_loc, s_ctx], axis=-1), sink).astype(v.dtype)
        return (jnp.einsum('bkgqs,bskd->bqkgd', p[..., :3 * BLOCK], vblk)
                + jnp.einsum('bkgqs,bskd->bqkgd', p[..., 3 * BLOCK:], vc))

    out = lax.map(one, (jnp.arange(nb), qb))
    return out.transpose(1, 0, 2, 3, 4, 5).reshape(B, L, ATTN_WIDTH)


def _chunk_gating(u, v, gain, ws, b):
    B, L, _ = u.shape
    u = jax.nn.gelu(u)
    v = _norm_gain(jax.nn.gelu(v), gain)
    vc = v.reshape(B, L // CHUNK, CHUNK, SGU_HEADS, SGU_HEAD_DIM)
    mixed = jnp.einsum('hij,bcjhd->bcihd', ws, vc) + b.T[:, :, None]
    return u * mixed.reshape(B, L, SGU_WIDTH)


def _short_conv(xin, bg, cg, w):
    L = xin.shape[1]
    hp = jnp.pad(cg * xin, ((0, 0), (1, 1), (0, 0)))
    y = w[0] * hp[:, :L] + w[1] * hp[:, 1:L + 1] + w[2] * hp[:, 2:L + 2]
    return bg * y


def _token_mixer(h, w_in, w_out, sink, sgu_gain, sgu_ws, sgu_b, conv_w, rope, ctx_kv):
    B, L, _ = h.shape
    proj = h @ w_in
    q, k, v, su, sv, cx, cb, cc = jnp.split(proj, IN_SPLITS, axis=-1)
    q = q.reshape(B, L, N_HEADS, HEAD_DIM)
    k = k.reshape(B, L, N_KV_HEADS, HEAD_DIM)
    v = v.reshape(B, L, N_KV_HEADS, HEAD_DIM)
    if ctx_kv is None:
        attn = _context_attention(q, k, v, sink)
        new_kv = (k, v)
    else:
        cos, sin = rope
        attn = _latent_attention(_apply_rope(q, cos, sin), _apply_rope(k, cos, sin), v,
                                 ctx_kv[0], ctx_kv[1], sink)
        new_kv = None
    sgu = _chunk_gating(su, sv, sgu_gain, sgu_ws, sgu_b)
    conv = _short_conv(cx, cb, cc, conv_w)
    mix = jnp.concatenate([attn, sgu, conv], axis=-1) @ w_out
    return mix, new_kv


def _layer(x, cvec, ada_w, ada_b, ffn1_wi, ffn1_wo, ffn2_wi, ffn2_wo, w_in, w_out, sink,
           sgu_gain, sgu_ws, sgu_b, conv_w, ln_g, ln_b, rope, ctx_kv):
    sh1, sc1, g1, sh2, sc2, g2, sh3, sc3, g3 = _modulation(cvec, ada_w, ada_b)
    h = x * (1.0 + sc1) + sh1
    x = _layer_norm(DEEPNORM_ALPHA * x + 0.5 * g1 * _swiglu(h, ffn1_wi, ffn1_wo), ln_g[0], ln_b[0])
    h = x * (1.0 + sc2) + sh2
    mix, new_kv = _token_mixer(h, w_in, w_out, sink, sgu_gain, sgu_ws, sgu_b, conv_w, rope, ctx_kv)
    x = _layer_norm(DEEPNORM_ALPHA * x + g2 * mix, ln_g[1], ln_b[1])
    h = x * (1.0 + sc3) + sh3
    x = _layer_norm(DEEPNORM_ALPHA * x + 0.5 * g3 * _swiglu(h, ffn2_wi, ffn2_wo), ln_g[2], ln_b[2])
    return x, new_kv


def setup_inputs(seed: int = 0) -> dict:
    key = jax.random.key(seed)
    ks = jax.random.split(key, 21)

    def nrm(k, shape, scale):
        return jax.random.normal(k, shape, jnp.float32) * scale

    return {
        'x_prompt': nrm(ks[0], (BATCH, SEQ, D_MODEL), 1.0),
        'x_sample': nrm(ks[1], (DEC_BATCH, DEC_SEQ, D_MODEL), 1.0),
        'cache_k': nrm(ks[2], (DEC_BATCH, DEPTH, PAST_LEN, N_KV_HEADS, HEAD_DIM), 1.0),
        'cache_v': nrm(ks[3], (DEC_BATCH, DEPTH, PAST_LEN, N_KV_HEADS, HEAD_DIM), 1.0),
        'c': nrm(ks[4], (DEC_BATCH, D_MODEL), 1.0),
        'c_ctx': nrm(ks[5], (D_MODEL,), 1.0),
        'ada_w': nrm(ks[6], (DEPTH, D_MODEL, N_MOD * D_MODEL), 0.5 * D_MODEL ** -0.5),
        'ada_b': nrm(ks[7], (DEPTH, N_MOD * D_MODEL), 0.01),
        'ffn1_wi': nrm(ks[8], (DEPTH, D_MODEL, 2 * D_FF), D_MODEL ** -0.5),
        'ffn1_wo': nrm(ks[9], (DEPTH, D_FF, D_MODEL), DEEPNORM_BETA * D_FF ** -0.5),
        'ffn2_wi': nrm(ks[10], (DEPTH, D_MODEL, 2 * D_FF), D_MODEL ** -0.5),
        'ffn2_wo': nrm(ks[11], (DEPTH, D_FF, D_MODEL), DEEPNORM_BETA * D_FF ** -0.5),
        'w_in': nrm(ks[12], (DEPTH, D_MODEL, IN_WIDTH), D_MODEL ** -0.5),
        'w_out': nrm(ks[13], (DEPTH, MIX_WIDTH, D_MODEL), DEEPNORM_BETA * MIX_WIDTH ** -0.5),
        'attn_sink': nrm(ks[14], (DEPTH, N_HEADS), 0.5),
        'sgu_gain': 1.0 + nrm(ks[15], (DEPTH, SGU_WIDTH), 0.02),
        'sgu_ws': nrm(ks[16], (DEPTH, SGU_HEADS, CHUNK, CHUNK), CHUNK ** -0.5),
        'sgu_b': 1.0 + nrm(ks[17], (DEPTH, SGU_HEADS, CHUNK), 0.02),
        'conv_w': nrm(ks[18], (DEPTH, CONV_K, CONV_WIDTH), CONV_K ** -0.5),
        'ln_g': 1.0 + nrm(ks[19], (DEPTH, 3, D_MODEL), 0.02),
        'ln_b': nrm(ks[20], (DEPTH, 3, D_MODEL), 0.02),
    }


def reference(x_prompt, x_sample, cache_k, cache_v, c, c_ctx, ada_w, ada_b, ffn1_wi, ffn1_wo,
              ffn2_wi, ffn2_wo, w_in, w_out, attn_sink, sgu_gain, sgu_ws, sgu_b, conv_w, ln_g, ln_b):
    rows = x_sample.shape[1] // GRID_W
    rope = _axial_rope(rows, x_sample.dtype)
    yp = x_prompt
    ys = x_sample
    ks_new = []
    vs_new = []
    for l in range(DEPTH):
        shared = (ada_w[l], ada_b[l], ffn1_wi[l], ffn1_wo[l], ffn2_wi[l], ffn2_wo[l], w_in[l], w_out[l],
                  attn_sink[l], sgu_gain[l], sgu_ws[l], sgu_b[l], conv_w[l], ln_g[l], ln_b[l])
        yp, kv = _layer(yp, c_ctx, *shared, None, None)
        ks_new.append(kv[0])
        vs_new.append(kv[1])
        ys, _ = _layer(ys, c, *shared, rope, (cache_k[:, l], cache_v[:, l]))
    new_cache_k = jnp.stack(ks_new, axis=1)
    new_cache_v = jnp.stack(vs_new, axis=1)
    return (yp, ys, new_cache_k, new_cache_v)
```

```python
import functools

import jax
import jax.numpy as jnp
from jax import lax
from jax.experimental import pallas as pl
from jax.experimental.pallas import tpu as pltpu

GRID_W = 64
BLOCK = 128
N_HEADS = 16
N_KV_HEADS = 4
GROUP = N_HEADS // N_KV_HEADS
HEAD_DIM = 64
ATTN_WIDTH = N_HEADS * HEAD_DIM
KV_WIDTH = N_KV_HEADS * HEAD_DIM
GROUP_WIDTH = GROUP * HEAD_DIM
ATTN_SCALE = HEAD_DIM ** -0.5
ROPE_BASE = 10000.0
SGU_HEADS = 4
SGU_HEAD_DIM = 128
SGU_WIDTH = SGU_HEADS * SGU_HEAD_DIM
CHUNK = 128
CONV_WIDTH = 512
MIX_WIDTH = ATTN_WIDTH + SGU_WIDTH + CONV_WIDTH
IN_WIDTH = ATTN_WIDTH + 2 * KV_WIDTH + 2 * SGU_WIDTH + 3 * CONV_WIDTH
N_MOD = 9
LN_EPS = 1e-5
NEG_INF = -1e30

K_COL256 = ATTN_WIDTH // KV_WIDTH
V_COL256 = K_COL256 + 1
SU_COL512 = (ATTN_WIDTH + 2 * KV_WIDTH) // 512
SV_COL512 = SU_COL512 + 1
CX_COL512 = SU_COL512 + 2
CB_COL512 = SU_COL512 + 3
CC_COL512 = SU_COL512 + 4

LANES = 128
SUBLANES = 8
MOD_ROWS = 16
VMEM_LIMIT_BYTES = 56 * 1024 * 1024

BF16 = jnp.bfloat16
F32 = jnp.float32


def _params(*sem):
    return pltpu.CompilerParams(dimension_semantics=sem, vmem_limit_bytes=VMEM_LIMIT_BYTES)


def _pick_tile(n, pref):
    t = min(n, pref)
    while n % t:
        t //= 2
    return t


def _layer_norm_rows(y, g, b):
    mu = jnp.mean(y, axis=-1, keepdims=True)
    d = y - mu
    var = jnp.mean(d * d, axis=-1, keepdims=True)
    return d * lax.rsqrt(var + LN_EPS) * g + b


def _mod_kernel(cv_ref, w_ref, b_ref, o_ref):
    a = jax.nn.silu(cv_ref[...]).astype(BF16)
    o_ref[...] = jnp.dot(a, w_ref[...].astype(BF16), preferred_element_type=F32) + b_ref[...]


def _modulation(cv, ada_w, ada_b):
    depth, d, nd = ada_w.shape
    tn = _pick_tile(nd, 1024)
    out = pl.pallas_call(
        _mod_kernel,
        grid=(depth, nd // tn),
        in_specs=[
            pl.BlockSpec((MOD_ROWS, d), lambda l, n: (0, 0)),
            pl.BlockSpec((None, d, tn), lambda l, n: (l, 0, n)),
            pl.BlockSpec((None, 1, tn), lambda l, n: (l, 0, n)),
        ],
        out_specs=pl.BlockSpec((None, MOD_ROWS, tn), lambda l, n: (l, 0, n)),
        out_shape=jax.ShapeDtypeStruct((depth, MOD_ROWS, nd), F32),
        compiler_params=_params("arbitrary", "arbitrary"),
        name="modulation",
    )(cv, ada_w, ada_b.reshape(depth, 1, nd))
    return out.reshape(depth, MOD_ROWS, N_MOD, d)


def _ffn_kernel(x_ref, mod_ref, lng_ref, lnb_ref, wig_ref, wiu_ref, wo_ref, o_ref, h_ref, acc_ref,
                *, piece, alpha, nj):
    j = pl.program_id(1)

    @pl.when(j == 0)
    def _():
        sh = mod_ref[3 * piece:3 * piece + 1, :]
        sc = mod_ref[3 * piece + 1:3 * piece + 2, :]
        h_ref[...] = (x_ref[...] * (1.0 + sc) + sh).astype(BF16)
        acc_ref[...] = jnp.zeros_like(acc_ref)

    h = h_ref[...]
    g = jnp.dot(h, wig_ref[...], preferred_element_type=F32)
    u = jnp.dot(h, wiu_ref[...], preferred_element_type=F32)
    a = (jax.nn.silu(g) * u).astype(BF16)
    acc_ref[...] += jnp.dot(a, wo_ref[...], preferred_element_type=F32)

    @pl.when(j == nj - 1)
    def _():
        gate = mod_ref[3 * piece + 2:3 * piece + 3, :]
        y = alpha * x_ref[...] + (0.5 * gate) * acc_ref[...]
        o_ref[...] = _layer_norm_rows(y, lng_ref[piece:piece + 1, :], lnb_ref[piece:piece + 1, :])


def _ffn(x, mod, ln_g, ln_b, wi, wo, *, layer, piece, alpha, row_of_tile, tm):
    t, d = x.shape
    f = wo.shape[1]
    tf = _pick_tile(f, 512)
    nj = f // tf
    l = layer
    return pl.pallas_call(
        functools.partial(_ffn_kernel, piece=piece, alpha=alpha, nj=nj),
        grid=(t // tm, nj),
        in_specs=[
            pl.BlockSpec((tm, d), lambda i, j: (i, 0)),
            pl.BlockSpec((None, None, N_MOD, d), lambda i, j: (l, row_of_tile(i, tm), 0, 0)),
            pl.BlockSpec((None, 3, d), lambda i, j: (l, 0, 0)),
            pl.BlockSpec((None, 3, d), lambda i, j: (l, 0, 0)),
            pl.BlockSpec((None, d, tf), lambda i, j: (l, 0, j)),
            pl.BlockSpec((None, d, tf), lambda i, j: (l, 0, j + nj)),
            pl.BlockSpec((None, tf, d), lambda i, j: (l, j, 0)),
        ],
        out_specs=pl.BlockSpec((tm, d), lambda i, j: (i, 0)),
        out_shape=jax.ShapeDtypeStruct((t, d), F32),
        scratch_shapes=[pltpu.VMEM((tm, d), BF16), pltpu.VMEM((tm, d), F32)],
        compiler_params=_params("arbitrary", "arbitrary"),
        name=f"ffn{piece}",
    )(x, mod, ln_g, ln_b, wi, wi, wo)


def _rope_slab(xs, cos, sin_signed):
    lane = lax.broadcasted_iota(jnp.int32, xs.shape, 1)
    partner = jnp.where(lane % 2 == 0, pltpu.roll(xs, LANES - 1, 1), pltpu.roll(xs, 1, 1))
    return xs * cos + partner * sin_signed


def _proj_kernel(x_ref, mod_ref, cos_ref, sin_ref, w_ref, o_ref, h_ref, *, n_ctx_tiles, tn):
    i = pl.program_id(0)
    j = pl.program_id(1)

    @pl.when(j == 0)
    def _():
        sh = mod_ref[3:4, :]
        sc = mod_ref[4:5, :]
        h_ref[...] = (x_ref[...] * (1.0 + sc) + sh).astype(BF16)

    o_ref[...] = jnp.dot(h_ref[...], w_ref[...], preferred_element_type=F32)

    rope_cols = ATTN_WIDTH + KV_WIDTH
    for jj in range(-(-rope_cols // tn)):
        n_slabs = (min(rope_cols, (jj + 1) * tn) - jj * tn) // LANES

        @pl.when(jnp.logical_and(i >= n_ctx_tiles, j == jj))
        def _(n_slabs=n_slabs):
            cos = cos_ref[...]
            sin = sin_ref[...]
            for s in range(n_slabs):
                cols = slice(s * LANES, (s + 1) * LANES)
                o_ref[:, cols] = _rope_slab(o_ref[:, cols], cos, sin)


def _proj(x, mod, cos_tab, sin_tab, w_in, *, layer, row_of_tile, tm, n_ctx_tokens, lat_seq):
    t, d = x.shape
    tn = 1024
    l = layer
    n_ctx_tiles = n_ctx_tokens // tm
    tiles_per_seq = lat_seq // tm

    def rope_idx(i, j):
        return (jnp.maximum(i - n_ctx_tiles, 0) % tiles_per_seq, 0)

    return pl.pallas_call(
        functools.partial(_proj_kernel, n_ctx_tiles=n_ctx_tiles, tn=tn),
        grid=(t // tm, IN_WIDTH // tn),
        in_specs=[
            pl.BlockSpec((tm, d), lambda i, j: (i, 0)),
            pl.BlockSpec((None, None, N_MOD, d), lambda i, j: (l, row_of_tile(i, tm), 0, 0)),
            pl.BlockSpec((tm, LANES), rope_idx),
            pl.BlockSpec((tm, LANES), rope_idx),
            pl.BlockSpec((None, d, tn), lambda i, j: (l, 0, j)),
        ],
        out_specs=pl.BlockSpec((tm, tn), lambda i, j: (i, j)),
        out_shape=jax.ShapeDtypeStruct((t, IN_WIDTH), F32),
        scratch_shapes=[pltpu.VMEM((tm, d), BF16)],
        compiler_params=_params("arbitrary", "arbitrary"),
        name="in_proj",
    )(x, mod, cos_tab, sin_tab, w_in)


def _lane_replicate(x, g):
    a = x[:, (g // 2) * LANES:(g // 2 + 1) * LANES]
    rolled = pltpu.roll(a, HEAD_DIM, 1)
    low = lax.broadcasted_iota(jnp.int32, a.shape, 1) < HEAD_DIM
    keep_low = (g % 2) == 0
    rep = jnp.where(low, a, rolled) if keep_low else jnp.where(low, rolled, a)
    rep = rep.astype(BF16)
    return jnp.concatenate([rep, rep], axis=1)


def _attn_kernel(sink_ref, q_ref, kp_ref, kc_ref, kn_ref, vp_ref, vc_ref, vn_ref, ck_ref, cv_ref, o_ref,
                 *, n_ctx_blocks, ctx_blocks_per_seq, lat_blocks_per_seq):
    n = pl.program_id(0)
    is_ctx = n < n_ctx_blocks
    pos = jnp.where(is_ctx, n % ctx_blocks_per_seq, (n - n_ctx_blocks) % lat_blocks_per_seq)
    last = jnp.where(is_ctx, ctx_blocks_per_seq - 1, lat_blocks_per_seq - 1)

    qi = lax.broadcasted_iota(jnp.int32, (BLOCK, 3 * BLOCK), 0)
    ki = lax.broadcasted_iota(jnp.int32, (BLOCK, 3 * BLOCK), 1) - BLOCK
    reach = jnp.where(is_ctx, 3 * BLOCK, BLOCK)
    k_lo = jnp.where(pos > 0, -BLOCK, 0)
    k_hi = jnp.where(pos < last, 2 * BLOCK - 1, BLOCK - 1)
    valid = (jnp.abs(qi - ki) <= reach) & (ki >= k_lo) & (ki <= k_hi)
    n_past = cv_ref.shape[0]
    past_ok = lax.broadcasted_iota(jnp.int32, (BLOCK, n_past), 1) < jnp.where(is_ctx, 0, n_past)

    k_loc = jnp.concatenate([kp_ref[...], kc_ref[...], kn_ref[...]], axis=0)
    v_loc = jnp.concatenate([vp_ref[...], vc_ref[...], vn_ref[...]], axis=0)
    k_past = ck_ref[...]
    v_past = cv_ref[...]

    lane = lax.broadcasted_iota(jnp.int32, (BLOCK, GROUP_WIDTH), 1)
    head_of_lane = lane // HEAD_DIM
    contract_last = (((1,), (1,)), ((), ()))

    for g in range(N_KV_HEADS):
        cols = slice(g * GROUP_WIDTH, (g + 1) * GROUP_WIDTH)
        qg = (q_ref[:, cols] * ATTN_SCALE).astype(BF16)
        q_stack = jnp.concatenate(
            [jnp.where(head_of_lane == h, qg, jnp.zeros_like(qg)) for h in range(GROUP)], axis=0)
        s_loc = lax.dot_general(q_stack, _lane_replicate(k_loc, g), contract_last, preferred_element_type=F32)
        s_past = lax.dot_general(q_stack, _lane_replicate(k_past, g), contract_last, preferred_element_type=F32)
        e_loc, e_past, inv = [], [], []
        for h in range(GROUP):
            rows = slice(h * BLOCK, (h + 1) * BLOCK)
            sl = jnp.where(valid, s_loc[rows], NEG_INF)
            sp = jnp.where(past_ok, s_past[rows], NEG_INF)
            sink = sink_ref[g * GROUP + h]
            m = jnp.maximum(jnp.maximum(jnp.max(sl, axis=-1, keepdims=True),
                                        jnp.max(sp, axis=-1, keepdims=True)), sink)
            el = jnp.exp(sl - m)
            ep = jnp.exp(sp - m)
            denom = (jnp.sum(el, axis=-1, keepdims=True) + jnp.sum(ep, axis=-1, keepdims=True)
                     + jnp.exp(sink - m))
            e_loc.append(el.astype(BF16))
            e_past.append(ep.astype(BF16))
            inv.append(1.0 / denom)
        pv = (jnp.dot(jnp.concatenate(e_loc, axis=0), _lane_replicate(v_loc, g), preferred_element_type=F32)
              + jnp.dot(jnp.concatenate(e_past, axis=0), _lane_replicate(v_past, g), preferred_element_type=F32))
        out = jnp.zeros((BLOCK, GROUP_WIDTH), F32)
        for h in range(GROUP):
            rows = slice(h * BLOCK, (h + 1) * BLOCK)
            out = jnp.where(head_of_lane == h, pv[rows] * inv[h], out)
        o_ref[:, cols] = out.astype(o_ref.dtype)


def _attention(proj, sink, cache_k, cache_v, *, layer, n_ctx_tokens, ctx_seq, lat_seq):
    t = proj.shape[0]
    nblk = t // BLOCK
    n_ctx_blocks = n_ctx_tokens // BLOCK
    cps = ctx_seq // BLOCK
    lps = lat_seq // BLOCK
    past = cache_k.shape[2]
    l = layer

    def seq_pos(n):
        return jnp.where(n < n_ctx_blocks, n % cps, (n - n_ctx_blocks) % lps)

    def prev_blk(n):
        return n - (seq_pos(n) > 0).astype(jnp.int32)

    def next_blk(n):
        last = jnp.where(n < n_ctx_blocks, cps - 1, lps - 1)
        return n + (seq_pos(n) < last).astype(jnp.int32)

    def past_idx(n):
        return (jnp.maximum(n - n_ctx_blocks, 0) // lps, l, 0, 0)

    kv_spec = lambda blk, col: pl.BlockSpec((BLOCK, KV_WIDTH), lambda n: (blk(n), col))
    same = lambda n: n
    return pl.pallas_call(
        functools.partial(_attn_kernel, n_ctx_blocks=n_ctx_blocks, ctx_blocks_per_seq=cps,
                          lat_blocks_per_seq=lps),
        grid=(nblk,),
        in_specs=[
            pl.BlockSpec(memory_space=pltpu.SMEM),
            pl.BlockSpec((BLOCK, ATTN_WIDTH), lambda n: (n, 0)),
            kv_spec(prev_blk, K_COL256), kv_spec(same, K_COL256), kv_spec(next_blk, K_COL256),
            kv_spec(prev_blk, V_COL256), kv_spec(same, V_COL256), kv_spec(next_blk, V_COL256),
            pl.BlockSpec((None, None, past, KV_WIDTH), past_idx),
            pl.BlockSpec((None, None, past, KV_WIDTH), past_idx),
        ],
        out_specs=pl.BlockSpec((BLOCK, ATTN_WIDTH), lambda n: (n, 0)),
        out_shape=jax.ShapeDtypeStruct((t, ATTN_WIDTH), BF16),
        compiler_params=_params("arbitrary"),
        name="attention",
    )(sink, proj, proj, proj, proj, proj, proj, proj, cache_k, cache_v)


def _mix_kernel(x_ref, mod_ref, lng_ref, lnb_ref, attn_ref, su_ref, sv_ref, cx_ref, cb_ref, cc_ref,
                cxp_ref, cxn_ref, ccp_ref, ccn_ref, gain_ref, ws_ref, sb_ref, cw_ref, wo_ref, o_ref, mix_ref,
                *, alpha, tm, n_ctx_tiles, ctx_tiles_per_seq, lat_tiles_per_seq):
    i = pl.program_id(0)
    is_ctx = i < n_ctx_tiles
    pos = jnp.where(is_ctx, i % ctx_tiles_per_seq, (i - n_ctx_tiles) % lat_tiles_per_seq)
    last = jnp.where(is_ctx, ctx_tiles_per_seq - 1, lat_tiles_per_seq - 1)

    mix_ref[:, :ATTN_WIDTH] = attn_ref[...]

    u = jax.nn.gelu(su_ref[...])
    v = jax.nn.gelu(sv_ref[...])
    mu = jnp.mean(v, axis=-1, keepdims=True)
    dv = v - mu
    var = jnp.mean(dv * dv, axis=-1, keepdims=True)
    vn = (dv * lax.rsqrt(var + LN_EPS) * gain_ref[...]).astype(BF16)
    for c in range(tm // CHUNK):
        rows = slice(c * CHUNK, (c + 1) * CHUNK)
        for h in range(SGU_HEADS):
            cols = slice(h * SGU_HEAD_DIM, (h + 1) * SGU_HEAD_DIM)
            mixed = jnp.dot(ws_ref[h], vn[rows, cols], preferred_element_type=F32) + sb_ref[:, cols]
            mix_ref[rows, ATTN_WIDTH + h * SGU_HEAD_DIM:ATTN_WIDTH + (h + 1) * SGU_HEAD_DIM] = (
                u[rows, cols] * mixed).astype(BF16)

    hp = cc_ref[...] * cx_ref[...]
    hp_prev = jnp.where(pos > 0, ccp_ref[SUBLANES - 1:SUBLANES, :] * cxp_ref[SUBLANES - 1:SUBLANES, :], 0.0)
    hp_next = jnp.where(pos < last, ccn_ref[0:1, :] * cxn_ref[0:1, :], 0.0)
    row = lax.broadcasted_iota(jnp.int32, hp.shape, 0)
    up = jnp.where(row == 0, hp_prev, pltpu.roll(hp, 1, 0))
    dn = jnp.where(row == tm - 1, hp_next, pltpu.roll(hp, tm - 1, 0))
    y = cw_ref[0:1, :] * up + cw_ref[1:2, :] * hp + cw_ref[2:3, :] * dn
    mix_ref[:, ATTN_WIDTH + SGU_WIDTH:] = (cb_ref[...] * y).astype(BF16)

    res = jnp.dot(mix_ref[...], wo_ref[...], preferred_element_type=F32)
    gate = mod_ref[5:6, :]
    yres = alpha * x_ref[...] + gate * res
    o_ref[...] = _layer_norm_rows(yres, lng_ref[1:2, :], lnb_ref[1:2, :])


def _mix(x, mod, ln_g, ln_b, attn, proj, gain, ws, sb_full, conv_w, w_out, *, layer, alpha, row_of_tile,
         n_ctx_tokens, ctx_seq, lat_seq):
    t, d = x.shape
    tm = _pick_tile(min(ctx_seq, lat_seq), 256)
    l = layer
    halo_per_tile = tm // SUBLANES
    n_halo_blocks = t // SUBLANES

    def col(c):
        return pl.BlockSpec((tm, 512), lambda i: (i, c))

    def halo_prev(c):
        return pl.BlockSpec((SUBLANES, 512), lambda i: (jnp.maximum(i * halo_per_tile - 1, 0), c))

    def halo_next(c):
        return pl.BlockSpec((SUBLANES, 512),
                            lambda i: (jnp.minimum((i + 1) * halo_per_tile, n_halo_blocks - 1), c))

    return pl.pallas_call(
        functools.partial(_mix_kernel, alpha=alpha, tm=tm, n_ctx_tiles=n_ctx_tokens // tm,
                          ctx_tiles_per_seq=ctx_seq // tm, lat_tiles_per_seq=lat_seq // tm),
        grid=(t // tm,),
        in_specs=[
            pl.BlockSpec((tm, d), lambda i: (i, 0)),
            pl.BlockSpec((None, None, N_MOD, d), lambda i: (l, row_of_tile(i, tm), 0, 0)),
            pl.BlockSpec((None, 3, d), lambda i: (l, 0, 0)),
            pl.BlockSpec((None, 3, d), lambda i: (l, 0, 0)),
            pl.BlockSpec((tm, ATTN_WIDTH), lambda i: (i, 0)),
            col(SU_COL512), col(SV_COL512), col(CX_COL512), col(CB_COL512), col(CC_COL512),
            halo_prev(CX_COL512), halo_next(CX_COL512), halo_prev(CC_COL512), halo_next(CC_COL512),
            pl.BlockSpec((None, 1, SGU_WIDTH), lambda i: (l, 0, 0)),
            pl.BlockSpec((None, SGU_HEADS, CHUNK, CHUNK), lambda i: (l, 0, 0, 0)),
            pl.BlockSpec((None, CHUNK, SGU_WIDTH), lambda i: (l, 0, 0)),
            pl.BlockSpec((None, 3, CONV_WIDTH), lambda i: (l, 0, 0)),
            pl.BlockSpec((None, MIX_WIDTH, d), lambda i: (l, 0, 0)),
        ],
        out_specs=pl.BlockSpec((tm, d), lambda i: (i, 0)),
        out_shape=jax.ShapeDtypeStruct((t, d), F32),
        scratch_shapes=[pltpu.VMEM((tm, MIX_WIDTH), BF16)],
        compiler_params=_params("arbitrary"),
        name="mix_out",
    )(x, mod, ln_g, ln_b, attn, proj, proj, proj, proj, proj, proj, proj, proj, proj,
      gain, ws, sb_full, conv_w, w_out)


def _rope_tables(lat_seq):
    pos = jnp.arange(lat_seq)
    row = (pos // GRID_W).astype(F32)
    col = (pos % GRID_W).astype(F32)
    n_freq = HEAD_DIM // 4
    inv = ROPE_BASE ** (-jnp.arange(n_freq, dtype=F32) / n_freq)
    ang = jnp.concatenate([row[:, None] * inv, col[:, None] * inv], axis=-1)
    cos = jnp.cos(ang)
    sin = jnp.sin(ang)
    cos_pairs = jnp.repeat(cos, 2, axis=-1)
    sin_pairs = jnp.stack([-sin, sin], axis=-1).reshape(lat_seq, HEAD_DIM)
    reps = LANES // HEAD_DIM
    return jnp.tile(cos_pairs, (1, reps)), jnp.tile(sin_pairs, (1, reps))


def kernel(x_prompt, x_sample, cache_k, cache_v, c, c_ctx, ada_w, ada_b, ffn1_wi, ffn1_wo, ffn2_wi, ffn2_wo,
           w_in, w_out, attn_sink, sgu_gain, sgu_ws, sgu_b, conv_w, ln_g, ln_b):
    batch, ctx_seq, d = x_prompt.shape
    lat_batch, lat_seq, _ = x_sample.shape
    depth = ada_w.shape[0]
    past = cache_k.shape[2]
    n_ctx_tokens = batch * ctx_seq
    assert lat_batch + 1 <= MOD_ROWS
    assert ctx_seq % BLOCK == 0 and lat_seq % BLOCK == 0 and lat_seq % GRID_W == 0

    alpha = (2 * depth) ** 0.25

    def row_of_tile(i, tm):
        first_lat = n_ctx_tokens // tm
        return jnp.where(i < first_lat, 0, 1 + jnp.maximum(i - first_lat, 0) // (lat_seq // tm))

    cv = jnp.zeros((MOD_ROWS, d), F32).at[0].set(c_ctx).at[1:1 + lat_batch].set(c)
    mod = _modulation(cv, ada_w, ada_b)

    wi1, wo1 = ffn1_wi.astype(BF16), ffn1_wo.astype(BF16)
    wi2, wo2 = ffn2_wi.astype(BF16), ffn2_wo.astype(BF16)
    w_in_b, w_out_b = w_in.astype(BF16), w_out.astype(BF16)
    ws_b = sgu_ws.astype(BF16)
    sb_full = jnp.repeat(jnp.swapaxes(sgu_b, 1, 2), SGU_HEAD_DIM, axis=2)
    gain = sgu_gain.reshape(depth, 1, SGU_WIDTH)
    cos_tab, sin_tab = _rope_tables(lat_seq)
    ck = cache_k.reshape(lat_batch, depth, past, KV_WIDTH)
    cvv = cache_v.reshape(lat_batch, depth, past, KV_WIDTH)

    x = jnp.concatenate([x_prompt.reshape(n_ctx_tokens, d), x_sample.reshape(lat_batch * lat_seq, d)], axis=0)
    tm = _pick_tile(min(n_ctx_tokens, lat_seq), 512)
    common = dict(row_of_tile=row_of_tile)
    ks_new, vs_new = [], []
    for l in range(depth):
        x = _ffn(x, mod, ln_g, ln_b, wi1, wo1, layer=l, piece=0, alpha=alpha, tm=tm, **common)
        proj = _proj(x, mod, cos_tab, sin_tab, w_in_b, layer=l, tm=tm, n_ctx_tokens=n_ctx_tokens,
                     lat_seq=lat_seq, **common)
        attn = _attention(proj, attn_sink[l], ck, cvv, layer=l, n_ctx_tokens=n_ctx_tokens,
                          ctx_seq=ctx_seq, lat_seq=lat_seq)
        x = _mix(x, mod, ln_g, ln_b, attn, proj, gain, ws_b, sb_full, conv_w, w_out_b, layer=l, alpha=alpha,
                 n_ctx_tokens=n_ctx_tokens, ctx_seq=ctx_seq, lat_seq=lat_seq, **common)
        x = _ffn(x, mod, ln_g, ln_b, wi2, wo2, layer=l, piece=2, alpha=alpha, tm=tm, **common)
        kv = proj[:n_ctx_tokens, ATTN_WIDTH:ATTN_WIDTH + 2 * KV_WIDTH]
        ks_new.append(kv[:, :KV_WIDTH].reshape(batch, ctx_seq, N_KV_HEADS, HEAD_DIM))
        vs_new.append(kv[:, KV_WIDTH:].reshape(batch, ctx_seq, N_KV_HEADS, HEAD_DIM))

    y_prompt = x[:n_ctx_tokens].reshape(batch, ctx_seq, d)
    y_sample = x[n_ctx_tokens:].reshape(lat_batch, lat_seq, d)
    return (y_prompt, y_sample, jnp.stack(ks_new, axis=1), jnp.stack(vs_new, axis=1))
```

```python
import functools

import jax
import jax.numpy as jnp
from jax import lax
from jax.experimental import pallas as pl
from jax.experimental.pallas import tpu as pltpu

GRID_W = 64
BLOCK = 128
N_HEADS = 16
N_KV_HEADS = 4
GROUP = N_HEADS // N_KV_HEADS
HEAD_DIM = 64
ATTN_WIDTH = N_HEADS * HEAD_DIM
KV_WIDTH = N_KV_HEADS * HEAD_DIM
GROUP_WIDTH = GROUP * HEAD_DIM
ATTN_SCALE = HEAD_DIM ** -0.5
ROPE_BASE = 10000.0
SGU_HEADS = 4
SGU_HEAD_DIM = 128
SGU_WIDTH = SGU_HEADS * SGU_HEAD_DIM
CHUNK = 128
CONV_WIDTH = 512
MIX_WIDTH = ATTN_WIDTH + SGU_WIDTH + CONV_WIDTH
IN_WIDTH = ATTN_WIDTH + 2 * KV_WIDTH + 2 * SGU_WIDTH + 3 * CONV_WIDTH
REST_WIDTH = 2 * SGU_WIDTH + 3 * CONV_WIDTH
N_MOD = 9
LN_EPS = 1e-5
NEG_INF = -1e30

LANES = 128
SUBLANES = 8
MOD_ROWS = 16
VMEM_LIMIT_BYTES = 56 * 1024 * 1024

BF16 = jnp.bfloat16
F32 = jnp.float32


def _params(*sem):
    return pltpu.CompilerParams(dimension_semantics=sem, vmem_limit_bytes=VMEM_LIMIT_BYTES)


def _pick_tile(n, pref):
    t = min(n, pref)
    while n % t:
        t //= 2
    return t


def _row_chunks(n, size):
    return [slice(r, r + size) for r in range(0, n, size)]


def _layer_norm_rows(y, g, b):
    mu = jnp.mean(y, axis=-1, keepdims=True)
    d = y - mu
    var = jnp.mean(d * d, axis=-1, keepdims=True)
    return d * lax.rsqrt(var + LN_EPS) * g + b


def _mod_kernel(cv_ref, w_ref, b_ref, o_ref):
    a = jax.nn.silu(cv_ref[...]).astype(BF16)
    o_ref[...] = jnp.dot(a, w_ref[...].astype(BF16), preferred_element_type=F32) + b_ref[...]


def _modulation(cv, ada_w, ada_b):
    depth, d, nd = ada_w.shape
    tn = _pick_tile(nd, 1024)
    out = pl.pallas_call(
        _mod_kernel,
        grid=(depth, nd // tn),
        in_specs=[
            pl.BlockSpec((MOD_ROWS, d), lambda l, n: (0, 0)),
            pl.BlockSpec((None, d, tn), lambda l, n: (l, 0, n)),
            pl.BlockSpec((None, 1, tn), lambda l, n: (l, 0, n)),
        ],
        out_specs=pl.BlockSpec((None, MOD_ROWS, tn), lambda l, n: (l, 0, n)),
        out_shape=jax.ShapeDtypeStruct((depth, MOD_ROWS, nd), F32),
        compiler_params=_params("arbitrary", "arbitrary"),
        name="modulation",
    )(cv, ada_w, ada_b.reshape(depth, 1, nd))
    return out.reshape(depth, MOD_ROWS, N_MOD, d)


def _ffn_kernel(x_ref, mod_ref, lng_ref, lnb_ref, wig_ref, wiu_ref, wo_ref, o_ref, h_ref, act_ref, acc_ref,
                *, piece, alpha, nj, chunk):
    j = pl.program_id(1)
    tm = x_ref.shape[0]

    def up(h):
        g = jnp.dot(h, wig_ref[...], preferred_element_type=F32)
        u = jnp.dot(h, wiu_ref[...], preferred_element_type=F32)
        return (jax.nn.silu(g) * u).astype(BF16)

    @pl.when(j == 0)
    def _():
        sh = mod_ref[3 * piece:3 * piece + 1, :]
        sc = mod_ref[3 * piece + 1:3 * piece + 2, :]
        acc_ref[...] = jnp.zeros_like(acc_ref)
        for rows in _row_chunks(tm, chunk):
            h = (x_ref[rows, :] * (1.0 + sc) + sh).astype(BF16)
            h_ref[rows, :] = h
            act_ref[0, rows, :] = up(h)

    @pl.when(jnp.logical_and(j > 0, j < nj))
    def _():
        slot = j % 2
        act_ref[slot] = up(h_ref[...])
        acc_ref[...] += jnp.dot(act_ref[1 - slot], wo_ref[...], preferred_element_type=F32)

    @pl.when(j == nj)
    def _():
        gate = 0.5 * mod_ref[3 * piece + 2:3 * piece + 3, :]
        g = lng_ref[piece:piece + 1, :]
        b = lnb_ref[piece:piece + 1, :]
        for rows in _row_chunks(tm, chunk):
            down = acc_ref[rows, :] + jnp.dot(act_ref[(nj - 1) % 2, rows, :], wo_ref[...],
                                              preferred_element_type=F32)
            o_ref[rows, :] = _layer_norm_rows(alpha * x_ref[rows, :] + gate * down, g, b)


def _ffn(x, mod, ln_g, ln_b, wi, wo, *, layer, piece, alpha, row_of_tile, tm):
    t, d = x.shape
    f = wo.shape[1]
    tf = _pick_tile(f, 512)
    nj = f // tf
    l = layer
    return pl.pallas_call(
        functools.partial(_ffn_kernel, piece=piece, alpha=alpha, nj=nj, chunk=_pick_tile(tm, 256)),
        grid=(t // tm, nj + 1),
        in_specs=[
            pl.BlockSpec((tm, d), lambda i, j: (i, 0)),
            pl.BlockSpec((None, None, N_MOD, d), lambda i, j: (l, row_of_tile(i, tm), 0, 0)),
            pl.BlockSpec((None, 3, d), lambda i, j: (l, 0, 0)),
            pl.BlockSpec((None, 3, d), lambda i, j: (l, 0, 0)),
            pl.BlockSpec((None, d, tf), lambda i, j: (l, 0, jnp.minimum(j, nj - 1))),
            pl.BlockSpec((None, d, tf), lambda i, j: (l, 0, jnp.minimum(j, nj - 1) + nj)),
            pl.BlockSpec((None, tf, d), lambda i, j: (l, jnp.maximum(j - 1, 0), 0)),
        ],
        out_specs=pl.BlockSpec((tm, d), lambda i, j: (i, 0)),
        out_shape=jax.ShapeDtypeStruct((t, d), F32),
        scratch_shapes=[pltpu.VMEM((tm, d), BF16), pltpu.VMEM((2, tm, tf), BF16), pltpu.VMEM((tm, d), F32)],
        compiler_params=_params("arbitrary", "arbitrary"),
        name=f"ffn{piece}",
    )(x, mod, ln_g, ln_b, wi, wi, wo)


def _rope_slab(xs, cos, sin_signed):
    lane = lax.broadcasted_iota(jnp.int32, xs.shape, 1)
    partner = jnp.where(lane % 2 == 0, pltpu.roll(xs, LANES - 1, 1), pltpu.roll(xs, 1, 1))
    return xs * cos + partner * sin_signed


def _proj_kernel(x_ref, mod_ref, cos_ref, sin_ref, w_ref, q_ref, k_ref, v_ref, kf_ref, vf_ref, rest_ref):
    sh = mod_ref[3:4, :]
    sc = mod_ref[4:5, :]
    h = (x_ref[...] * (1.0 + sc) + sh).astype(BF16)
    cos = cos_ref[...]
    sin = sin_ref[...]

    q = jnp.dot(h, w_ref[:, :ATTN_WIDTH], preferred_element_type=F32)
    for s in range(ATTN_WIDTH // LANES):
        cols = slice(s * LANES, (s + 1) * LANES)
        q_ref[:, cols] = (_rope_slab(q[:, cols], cos, sin) * ATTN_SCALE).astype(BF16)

    kv = jnp.dot(h, w_ref[:, ATTN_WIDTH:ATTN_WIDTH + 2 * KV_WIDTH], preferred_element_type=F32)
    for s in range(KV_WIDTH // LANES):
        cols = slice(s * LANES, (s + 1) * LANES)
        k = _rope_slab(kv[:, cols], cos, sin)
        kf_ref[:, cols] = k
        k_ref[:, cols] = k.astype(BF16)
    v = kv[:, KV_WIDTH:]
    vf_ref[...] = v
    v_ref[...] = v.astype(BF16)

    rest_ref[...] = jnp.dot(h, w_ref[:, ATTN_WIDTH + 2 * KV_WIDTH:], preferred_element_type=F32)


def _proj(x, mod, cos_tab, sin_tab, w_in, *, layer, row_of_tile, tm, n_ctx_tokens, lat_seq):
    t, d = x.shape
    l = layer
    n_ctx_tiles = n_ctx_tokens // tm
    tiles_per_seq = lat_seq // tm

    def rope_idx(i):
        return (jnp.where(i < n_ctx_tiles, 0, 1 + jnp.maximum(i - n_ctx_tiles, 0) % tiles_per_seq), 0)

    def rows(width):
        return pl.BlockSpec((tm, width), lambda i: (i, 0))

    return pl.pallas_call(
        _proj_kernel,
        grid=(t // tm,),
        in_specs=[
            rows(d),
            pl.BlockSpec((None, None, N_MOD, d), lambda i: (l, row_of_tile(i, tm), 0, 0)),
            pl.BlockSpec((tm, LANES), rope_idx),
            pl.BlockSpec((tm, LANES), rope_idx),
            pl.BlockSpec((None, d, IN_WIDTH), lambda i: (l, 0, 0), pipeline_mode=pl.Buffered(1)),
        ],
        out_specs=[rows(ATTN_WIDTH), rows(KV_WIDTH), rows(KV_WIDTH), rows(KV_WIDTH), rows(KV_WIDTH),
                   rows(REST_WIDTH)],
        out_shape=[
            jax.ShapeDtypeStruct((t, ATTN_WIDTH), BF16),
            jax.ShapeDtypeStruct((t, KV_WIDTH), BF16),
            jax.ShapeDtypeStruct((t, KV_WIDTH), BF16),
            jax.ShapeDtypeStruct((t, KV_WIDTH), F32),
            jax.ShapeDtypeStruct((t, KV_WIDTH), F32),
            jax.ShapeDtypeStruct((t, REST_WIDTH), F32),
        ],
        compiler_params=_params("arbitrary"),
        name="in_proj",
    )(x, mod, cos_tab, sin_tab, w_in)


def _lane_replicate(x, g):
    kg = x[:, g * HEAD_DIM:(g + 1) * HEAD_DIM].astype(BF16)
    return jnp.concatenate([kg] * GROUP, axis=1)


def _attend(sink_ref, q_ref, o_ref, k_loc, v_loc, valid, past):
    lane = lax.broadcasted_iota(jnp.int32, (BLOCK, GROUP_WIDTH), 1)
    head_of_lane = lane // HEAD_DIM
    contract_last = (((1,), (1,)), ((), ()))
    for g in range(N_KV_HEADS):
        cols = slice(g * GROUP_WIDTH, (g + 1) * GROUP_WIDTH)
        qg = q_ref[:, cols]
        q_stack = jnp.concatenate(
            [jnp.where(head_of_lane == h, qg, jnp.zeros_like(qg)) for h in range(GROUP)], axis=0)
        s_loc = lax.dot_general(q_stack, _lane_replicate(k_loc, g), contract_last, preferred_element_type=F32)
        if past is not None:
            s_past = lax.dot_general(q_stack, past[0][g], contract_last, preferred_element_type=F32)
        e_loc, e_past, inv = [], [], []
        for h in range(GROUP):
            rows = slice(h * BLOCK, (h + 1) * BLOCK)
            sink = sink_ref[g * GROUP + h]
            sl = jnp.where(valid, s_loc[rows], NEG_INF)
            m = jnp.maximum(jnp.max(sl, axis=-1, keepdims=True), sink)
            if past is not None:
                sp = s_past[rows]
                m = jnp.maximum(m, jnp.max(sp, axis=-1, keepdims=True))
            el = jnp.exp(sl - m)
            denom = jnp.sum(el, axis=-1, keepdims=True) + jnp.exp(sink - m)
            e_loc.append(el.astype(BF16))
            if past is not None:
                ep = jnp.exp(sp - m)
                denom = denom + jnp.sum(ep, axis=-1, keepdims=True)
                e_past.append(ep.astype(BF16))
            inv.append(1.0 / denom)
        pv = jnp.dot(jnp.concatenate(e_loc, axis=0), _lane_replicate(v_loc, g), preferred_element_type=F32)
        if past is not None:
            pv = pv + jnp.dot(jnp.concatenate(e_past, axis=0), past[1][g], preferred_element_type=F32)
        out = jnp.zeros((BLOCK, GROUP_WIDTH), F32)
        for h in range(GROUP):
            rows = slice(h * BLOCK, (h + 1) * BLOCK)
            out = jnp.where(head_of_lane == h, pv[rows] * inv[h], out)
        o_ref[:, cols] = out.astype(o_ref.dtype)


def _attn_kernel(sink_ref, q_ref, kp_ref, kc_ref, kn_ref, vp_ref, vc_ref, vn_ref, ck_ref, cv_ref, o_ref,
                 kpast_ref, vpast_ref, *, n_ctx_blocks, ctx_blocks_per_seq, lat_blocks_per_seq):
    n = pl.program_id(0)
    is_ctx = n < n_ctx_blocks
    pos = jnp.where(is_ctx, n % ctx_blocks_per_seq, (n - n_ctx_blocks) % lat_blocks_per_seq)
    last = jnp.where(is_ctx, ctx_blocks_per_seq - 1, lat_blocks_per_seq - 1)

    qi = lax.broadcasted_iota(jnp.int32, (BLOCK, 3 * BLOCK), 0)
    ki = lax.broadcasted_iota(jnp.int32, (BLOCK, 3 * BLOCK), 1) - BLOCK
    k_lo = jnp.where(pos > 0, -BLOCK, 0)
    k_hi = jnp.where(pos < last, 2 * BLOCK - 1, BLOCK - 1)
    in_seq = (ki >= k_lo) & (ki <= k_hi)

    k_loc = jnp.concatenate([kp_ref[...], kc_ref[...], kn_ref[...]], axis=0)
    v_loc = jnp.concatenate([vp_ref[...], vc_ref[...], vn_ref[...]], axis=0)

    @pl.when(is_ctx)
    def _():
        _attend(sink_ref, q_ref, o_ref, k_loc, v_loc, in_seq, None)

    @pl.when(jnp.logical_and(jnp.logical_not(is_ctx), pos == 0))
    def _():
        for g in range(N_KV_HEADS):
            kpast_ref[g] = _lane_replicate(ck_ref[...], g)
            vpast_ref[g] = _lane_replicate(cv_ref[...], g)

    @pl.when(jnp.logical_not(is_ctx))
    def _():
        band = in_seq & (jnp.abs(qi - ki) <= BLOCK)
        _attend(sink_ref, q_ref, o_ref, k_loc, v_loc, band, (kpast_ref, vpast_ref))


def _attention(q, k, v, sink, cache_k, cache_v, *, layer, n_ctx_tokens, ctx_seq, lat_seq):
    t = q.shape[0]
    nblk = t // BLOCK
    n_ctx_blocks = n_ctx_tokens // BLOCK
    cps = ctx_seq // BLOCK
    lps = lat_seq // BLOCK
    past = cache_k.shape[2]
    l = layer

    def seq_pos(n):
        return jnp.where(n < n_ctx_blocks, n % cps, (n - n_ctx_blocks) % lps)

    def prev_blk(n):
        return (n - (seq_pos(n) > 0).astype(jnp.int32), 0)

    def next_blk(n):
        last = jnp.where(n < n_ctx_blocks, cps - 1, lps - 1)
        return (n + (seq_pos(n) < last).astype(jnp.int32), 0)

    def past_idx(n):
        return (jnp.maximum(n - n_ctx_blocks, 0) // lps, l, 0, 0)

    same = lambda n: (n, 0)
    kv_spec = lambda idx: pl.BlockSpec((BLOCK, KV_WIDTH), idx)
    return pl.pallas_call(
        functools.partial(_attn_kernel, n_ctx_blocks=n_ctx_blocks, ctx_blocks_per_seq=cps,
                          lat_blocks_per_seq=lps),
        grid=(nblk,),
        in_specs=[
            pl.BlockSpec(memory_space=pltpu.SMEM),
            pl.BlockSpec((BLOCK, ATTN_WIDTH), same),
            kv_spec(prev_blk), kv_spec(same), kv_spec(next_blk),
            kv_spec(prev_blk), kv_spec(same), kv_spec(next_blk),
            pl.BlockSpec((None, None, past, KV_WIDTH), past_idx),
            pl.BlockSpec((None, None, past, KV_WIDTH), past_idx),
        ],
        out_specs=pl.BlockSpec((BLOCK, ATTN_WIDTH), same),
        out_shape=jax.ShapeDtypeStruct((t, ATTN_WIDTH), BF16),
        scratch_shapes=[pltpu.VMEM((N_KV_HEADS, past, GROUP_WIDTH), BF16),
                        pltpu.VMEM((N_KV_HEADS, past, GROUP_WIDTH), BF16)],
        compiler_params=_params("arbitrary"),
        name="attention",
    )(sink, q, k, k, k, v, v, v, cache_k, cache_v)


def _mix_kernel(x_ref, mod_ref, lng_ref, lnb_ref, attn_ref, su_ref, sv_ref, cx_ref, cb_ref, cc_ref,
                cxp_ref, cxn_ref, ccp_ref, ccn_ref, gain_ref, ws_ref, sb_ref, cw_ref, wo_ref, o_ref,
                *, alpha, tm, ctx_seq, n_ctx_tiles, lat_tiles_per_seq):
    i = pl.program_id(0)
    is_ctx = i < n_ctx_tiles
    ctx_tiles_per_seq = max(ctx_seq // tm, 1)
    pos = jnp.where(is_ctx, i % ctx_tiles_per_seq, (i - n_ctx_tiles) % lat_tiles_per_seq)
    last = jnp.where(is_ctx, ctx_tiles_per_seq - 1, lat_tiles_per_seq - 1)

    hp = cc_ref[...] * cx_ref[...]
    hp_prev = jnp.where(pos > 0, ccp_ref[SUBLANES - 1:SUBLANES, :] * cxp_ref[SUBLANES - 1:SUBLANES, :], 0.0)
    hp_next = jnp.where(pos < last, ccn_ref[0:1, :] * cxn_ref[0:1, :], 0.0)
    row = lax.broadcasted_iota(jnp.int32, hp.shape, 0)
    if ctx_seq < tm:
        row = jnp.where(is_ctx, row & (ctx_seq - 1), row)
    last_row = jnp.where(is_ctx, min(ctx_seq, tm) - 1, tm - 1)
    up = jnp.where(row == 0, hp_prev, pltpu.roll(hp, 1, 0))
    dn = jnp.where(row == last_row, hp_next, pltpu.roll(hp, tm - 1, 0))
    y = cw_ref[0:1, :] * up + cw_ref[1:2, :] * hp + cw_ref[2:3, :] * dn
    conv = (cb_ref[...] * y).astype(BF16)

    gate = mod_ref[5:6, :]
    g = lng_ref[1:2, :]
    b = lnb_ref[1:2, :]
    for rows in _row_chunks(tm, CHUNK):
        u = jax.nn.gelu(su_ref[rows, :])
        v = jax.nn.gelu(sv_ref[rows, :])
        mu = jnp.mean(v, axis=-1, keepdims=True)
        dv = v - mu
        var = jnp.mean(dv * dv, axis=-1, keepdims=True)
        vn = (dv * lax.rsqrt(var + LN_EPS) * gain_ref[...]).astype(BF16)
        parts = [attn_ref[rows, :]]
        for h in range(SGU_HEADS):
            cols = slice(h * SGU_HEAD_DIM, (h + 1) * SGU_HEAD_DIM)
            mixed = jnp.dot(ws_ref[h], vn[:, cols], preferred_element_type=F32) + sb_ref[:, cols]
            parts.append((u[:, cols] * mixed).astype(BF16))
        parts.append(conv[rows, :])
        res = jnp.dot(jnp.concatenate(parts, axis=1), wo_ref[...], preferred_element_type=F32)
        o_ref[rows, :] = _layer_norm_rows(alpha * x_ref[rows, :] + gate * res, g, b)


def _mix(x, mod, ln_g, ln_b, attn, rest, gain, ws, sb_full, conv_w, w_out, *, layer, alpha, row_of_tile,
         n_ctx_tokens, ctx_seq, lat_seq):
    t, d = x.shape
    tm = _pick_tile(min(n_ctx_tokens, lat_seq), 512)
    l = layer
    halo_per_tile = tm // SUBLANES
    n_halo_blocks = t // SUBLANES
    su_c, sv_c, cx_c, cb_c, cc_c = range(5)
    assert lat_seq % tm == 0 and (ctx_seq % tm == 0 or (tm % ctx_seq == 0 and ctx_seq & (ctx_seq - 1) == 0))

    def col(c):
        return pl.BlockSpec((tm, 512), lambda i: (i, c))

    def halo_prev(c):
        return pl.BlockSpec((SUBLANES, 512), lambda i: (jnp.maximum(i * halo_per_tile - 1, 0), c))

    def halo_next(c):
        return pl.BlockSpec((SUBLANES, 512),
                            lambda i: (jnp.minimum((i + 1) * halo_per_tile, n_halo_blocks - 1), c))

    return pl.pallas_call(
        functools.partial(_mix_kernel, alpha=alpha, tm=tm, ctx_seq=ctx_seq, n_ctx_tiles=n_ctx_tokens // tm,
                          lat_tiles_per_seq=lat_seq // tm),
        grid=(t // tm,),
        in_specs=[
            pl.BlockSpec((tm, d), lambda i: (i, 0)),
            pl.BlockSpec((None, None, N_MOD, d), lambda i: (l, row_of_tile(i, tm), 0, 0)),
            pl.BlockSpec((None, 3, d), lambda i: (l, 0, 0)),
            pl.BlockSpec((None, 3, d), lambda i: (l, 0, 0)),
            pl.BlockSpec((tm, ATTN_WIDTH), lambda i: (i, 0)),
            col(su_c), col(sv_c), col(cx_c), col(cb_c), col(cc_c),
            halo_prev(cx_c), halo_next(cx_c), halo_prev(cc_c), halo_next(cc_c),
            pl.BlockSpec((None, 1, SGU_WIDTH), lambda i: (l, 0, 0)),
            pl.BlockSpec((None, SGU_HEADS, CHUNK, CHUNK), lambda i: (l, 0, 0, 0)),
            pl.BlockSpec((None, CHUNK, SGU_WIDTH), lambda i: (l, 0, 0)),
            pl.BlockSpec((None, 3, CONV_WIDTH), lambda i: (l, 0, 0)),
            pl.BlockSpec((None, MIX_WIDTH, d), lambda i: (l, 0, 0), pipeline_mode=pl.Buffered(1)),
        ],
        out_specs=pl.BlockSpec((tm, d), lambda i: (i, 0)),
        out_shape=jax.ShapeDtypeStruct((t, d), F32),
        compiler_params=_params("arbitrary"),
        name="mix_out",
    )(x, mod, ln_g, ln_b, attn, rest, rest, rest, rest, rest, rest, rest, rest, rest,
      gain, ws, sb_full, conv_w, w_out)


def _rope_tables(lat_seq, identity_rows):
    pos = jnp.arange(lat_seq)
    row = (pos // GRID_W).astype(F32)
    col = (pos % GRID_W).astype(F32)
    n_freq = HEAD_DIM // 4
    inv = ROPE_BASE ** (-jnp.arange(n_freq, dtype=F32) / n_freq)
    ang = jnp.concatenate([row[:, None] * inv, col[:, None] * inv], axis=-1)
    cos = jnp.cos(ang)
    sin = jnp.sin(ang)
    cos_pairs = jnp.repeat(cos, 2, axis=-1)
    sin_pairs = jnp.stack([-sin, sin], axis=-1).reshape(lat_seq, HEAD_DIM)
    reps = LANES // HEAD_DIM
    cos_tab = jnp.concatenate([jnp.ones((identity_rows, LANES), F32), jnp.tile(cos_pairs, (1, reps))], axis=0)
    sin_tab = jnp.concatenate([jnp.zeros((identity_rows, LANES), F32), jnp.tile(sin_pairs, (1, reps))], axis=0)
    return cos_tab, sin_tab


def kernel(x_prompt, x_sample, cache_k, cache_v, c, c_ctx, ada_w, ada_b, ffn1_wi, ffn1_wo, ffn2_wi, ffn2_wo,
           w_in, w_out, attn_sink, sgu_gain, sgu_ws, sgu_b, conv_w, ln_g, ln_b):
    batch, ctx_seq, d = x_prompt.shape
    lat_batch, lat_seq, _ = x_sample.shape
    depth = ada_w.shape[0]
    past = cache_k.shape[2]
    n_ctx_tokens = batch * ctx_seq
    assert lat_batch + 1 <= MOD_ROWS
    assert ctx_seq % BLOCK == 0 and lat_seq % BLOCK == 0 and lat_seq % GRID_W == 0

    alpha = (2 * depth) ** 0.25

    def row_of_tile(i, tm):
        first_lat = n_ctx_tokens // tm
        return jnp.where(i < first_lat, 0, 1 + jnp.maximum(i - first_lat, 0) // (lat_seq // tm))

    cv = jnp.zeros((MOD_ROWS, d), F32).at[0].set(c_ctx).at[1:1 + lat_batch].set(c)
    mod = _modulation(cv, ada_w, ada_b)

    wi1, wo1 = ffn1_wi.astype(BF16), ffn1_wo.astype(BF16)
    wi2, wo2 = ffn2_wi.astype(BF16), ffn2_wo.astype(BF16)
    w_in_b, w_out_b = w_in.astype(BF16), w_out.astype(BF16)
    ws_b = sgu_ws.astype(BF16)
    sb_full = jnp.repeat(jnp.swapaxes(sgu_b, 1, 2), SGU_HEAD_DIM, axis=2)
    gain = sgu_gain.reshape(depth, 1, SGU_WIDTH)
    ck = cache_k.reshape(lat_batch, depth, past, KV_WIDTH)
    cvv = cache_v.reshape(lat_batch, depth, past, KV_WIDTH)

    x = jnp.concatenate([x_prompt.reshape(n_ctx_tokens, d), x_sample.reshape(lat_batch * lat_seq, d)], axis=0)
    tm = _pick_tile(min(n_ctx_tokens, lat_seq), 512)
    cos_tab, sin_tab = _rope_tables(lat_seq, tm)
    common = dict(row_of_tile=row_of_tile)
    ks_new, vs_new = [], []
    for l in range(depth):
        x = _ffn(x, mod, ln_g, ln_b, wi1, wo1, layer=l, piece=0, alpha=alpha, tm=tm, **common)
        q, k, v, kf, vf, rest = _proj(x, mod, cos_tab, sin_tab, w_in_b, layer=l, tm=tm,
                                      n_ctx_tokens=n_ctx_tokens, lat_seq=lat_seq, **common)
        attn = _attention(q, k, v, attn_sink[l], ck, cvv, layer=l, n_ctx_tokens=n_ctx_tokens,
                          ctx_seq=ctx_seq, lat_seq=lat_seq)
        x = _mix(x, mod, ln_g, ln_b, attn, rest, gain, ws_b, sb_full, conv_w, w_out_b, layer=l, alpha=alpha,
                 n_ctx_tokens=n_ctx_tokens, ctx_seq=ctx_seq, lat_seq=lat_seq, **common)
        x = _ffn(x, mod, ln_g, ln_b, wi2, wo2, layer=l, piece=2, alpha=alpha, tm=tm, **common)
        ks_new.append(kf[:n_ctx_tokens].reshape(batch, ctx_seq, N_KV_HEADS, HEAD_DIM))
        vs_new.append(vf[:n_ctx_tokens].reshape(batch, ctx_seq, N_KV_HEADS, HEAD_DIM))

    y_prompt = x[:n_ctx_tokens].reshape(batch, ctx_seq, d)
    y_sample = x[n_ctx_tokens:].reshape(lat_batch, lat_seq, d)
    return (y_prompt, y_sample, jnp.stack(ks_new, axis=1), jnp.stack(vs_new, axis=1))
```

```python
import functools

import jax
import jax.numpy as jnp
from jax import lax
from jax.experimental import pallas as pl
from jax.experimental.pallas import tpu as pltpu

GRID_W = 64
BLOCK = 128
N_HEADS = 16
N_KV_HEADS = 4
GROUP = N_HEADS // N_KV_HEADS
HEAD_DIM = 64
ATTN_WIDTH = N_HEADS * HEAD_DIM
KV_WIDTH = N_KV_HEADS * HEAD_DIM
GROUP_WIDTH = GROUP * HEAD_DIM
ATTN_SCALE = HEAD_DIM ** -0.5
ROPE_BASE = 10000.0
SGU_HEADS = 4
SGU_HEAD_DIM = 128
SGU_WIDTH = SGU_HEADS * SGU_HEAD_DIM
CHUNK = 128
CONV_WIDTH = 512
MIX_WIDTH = ATTN_WIDTH + SGU_WIDTH + CONV_WIDTH
IN_WIDTH = ATTN_WIDTH + 2 * KV_WIDTH + 2 * SGU_WIDTH + 3 * CONV_WIDTH
REST_WIDTH = 2 * SGU_WIDTH + 3 * CONV_WIDTH
N_MOD = 9
LN_EPS = 1e-5
NEG_INF = -1e30

LANES = 128
SUBLANES = 8
MXU_WIDTH = 256
MOD_ROWS = 16
VMEM_LIMIT_BYTES = 56 * 1024 * 1024

BF16 = jnp.bfloat16
F32 = jnp.float32


def _params(*sem):
    return pltpu.CompilerParams(dimension_semantics=sem, vmem_limit_bytes=VMEM_LIMIT_BYTES)


def _pick_tile(n, pref):
    t = min(n, pref)
    while n % t:
        t //= 2
    return t


def _row_chunks(n, size):
    return [slice(r, r + size) for r in range(0, n, size)]


def _layer_norm_rows(y, g, b):
    mu = jnp.mean(y, axis=-1, keepdims=True)
    d = y - mu
    var = jnp.mean(d * d, axis=-1, keepdims=True)
    return d * lax.rsqrt(var + LN_EPS) * g + b


def _mod_kernel(cv_ref, w_ref, b_ref, o_ref):
    a = jax.nn.silu(cv_ref[...]).astype(BF16)
    o_ref[...] = jnp.dot(a, w_ref[...].astype(BF16), preferred_element_type=F32) + b_ref[...]


def _modulation(cv, ada_w, ada_b):
    depth, d, nd = ada_w.shape
    tn = _pick_tile(nd, 1024)
    out = pl.pallas_call(
        _mod_kernel,
        grid=(depth, nd // tn),
        in_specs=[
            pl.BlockSpec((MOD_ROWS, d), lambda l, n: (0, 0)),
            pl.BlockSpec((None, d, tn), lambda l, n: (l, 0, n)),
            pl.BlockSpec((None, 1, tn), lambda l, n: (l, 0, n)),
        ],
        out_specs=pl.BlockSpec((None, MOD_ROWS, tn), lambda l, n: (l, 0, n)),
        out_shape=jax.ShapeDtypeStruct((depth, MOD_ROWS, nd), F32),
        compiler_params=_params("arbitrary", "arbitrary"),
        name="modulation",
    )(cv, ada_w, ada_b.reshape(depth, 1, nd))
    return out.reshape(depth, MOD_ROWS, N_MOD, d)


def _ffn_kernel(x_ref, mod_ref, lng_ref, lnb_ref, wig_ref, wiu_ref, wo_ref, *rest, piece, alpha, nj, chunk):
    o_ref, h_ref, acc_ref = rest[-3:]
    j = pl.program_id(1)
    tm = x_ref.shape[0]

    def modulated(rows):
        sh = mod_ref[3 * piece:3 * piece + 1, :]
        sc = mod_ref[3 * piece + 1:3 * piece + 2, :]
        return (x_ref[rows, :] * (1.0 + sc) + sh).astype(BF16)

    def up(h):
        tf = wig_ref.shape[1]
        parts = []
        for c in range(0, tf, MXU_WIDTH):
            cols = slice(c, min(c + MXU_WIDTH, tf))
            g = jnp.dot(h, wig_ref[:, cols], preferred_element_type=F32)
            u = jnp.dot(h, wiu_ref[:, cols], preferred_element_type=F32)
            parts.append((jax.nn.silu(g) * u).astype(BF16))
        return jnp.concatenate(parts, axis=1)

    def down(a):
        return jnp.dot(a, wo_ref[...], preferred_element_type=F32)

    def finish(rows, total):
        gate = 0.5 * mod_ref[3 * piece + 2:3 * piece + 3, :]
        y = alpha * x_ref[rows, :] + gate * total
        o_ref[rows, :] = _layer_norm_rows(y, lng_ref[piece:piece + 1, :], lnb_ref[piece:piece + 1, :])

    if nj == 1:
        for rows in _row_chunks(tm, chunk):
            finish(rows, down(up(modulated(rows))))
        return

    @pl.when(j == 0)
    def _():
        h = modulated(slice(0, tm))
        h_ref[...] = h
        acc_ref[...] = down(up(h))

    @pl.when(jnp.logical_and(j > 0, j < nj - 1))
    def _():
        acc_ref[...] += down(up(h_ref[...]))

    @pl.when(j == nj - 1)
    def _():
        for rows in _row_chunks(tm, chunk):
            finish(rows, acc_ref[rows, :] + down(up(h_ref[rows, :])))


def _ffn(x, mod, ln_g, ln_b, wi, wo, *, layer, piece, alpha, row_of_tile, tm, first_tile=0, n_tiles=None,
         x_first_tile=0, out_first_tile=0, out_tiles=None, prev=None):
    d = x.shape[1]
    f = wo.shape[1]
    tf = _pick_tile(f, 512)
    nj = f // tf
    l = layer
    n_tiles = x.shape[0] // tm if n_tiles is None else n_tiles
    out_tiles = n_tiles if out_tiles is None else out_tiles
    in_specs = [
        pl.BlockSpec((tm, d), lambda i, j: (i + first_tile - x_first_tile, 0)),
        pl.BlockSpec((None, None, N_MOD, d), lambda i, j: (l, row_of_tile(i + first_tile, tm), 0, 0)),
        pl.BlockSpec((None, 3, d), lambda i, j: (l, 0, 0)),
        pl.BlockSpec((None, 3, d), lambda i, j: (l, 0, 0)),
        pl.BlockSpec((None, d, tf), lambda i, j: (l, 0, j)),
        pl.BlockSpec((None, d, tf), lambda i, j: (l, 0, j + nj)),
        pl.BlockSpec((None, tf, d), lambda i, j: (l, j, 0)),
    ]
    args = [x, mod, ln_g, ln_b, wi, wi, wo]
    aliases = {}
    if prev is not None:
        in_specs.append(pl.BlockSpec(memory_space=pl.ANY))
        args.append(prev)
        aliases = {len(args) - 1: 0}
    return pl.pallas_call(
        functools.partial(_ffn_kernel, piece=piece, alpha=alpha, nj=nj, chunk=_pick_tile(tm, 256)),
        grid=(n_tiles, nj),
        in_specs=in_specs,
        out_specs=pl.BlockSpec((tm, d), lambda i, j: (i + first_tile - out_first_tile, 0)),
        out_shape=jax.ShapeDtypeStruct((out_tiles * tm, d), F32),
        scratch_shapes=[pltpu.VMEM((tm, d), BF16), pltpu.VMEM((tm, d), F32)],
        input_output_aliases=aliases,
        compiler_params=_params("arbitrary", "arbitrary"),
        name=f"ffn{piece}",
    )(*args)


def _rope_slab(xs, cos, sin_signed):
    lane = lax.broadcasted_iota(jnp.int32, xs.shape, 1)
    partner = jnp.where(lane % 2 == 0, pltpu.roll(xs, LANES - 1, 1), pltpu.roll(xs, 1, 1))
    return xs * cos + partner * sin_signed


def _proj_kernel(x_ref, mod_ref, cos_ref, sin_ref, w_ref, q_ref, kv_ref, kf_ref, vf_ref, rest_ref):
    sh = mod_ref[3:4, :]
    sc = mod_ref[4:5, :]
    h = (x_ref[...] * (1.0 + sc) + sh).astype(BF16)
    cos = cos_ref[...]
    sin = sin_ref[...]

    q = jnp.dot(h, w_ref[:, :ATTN_WIDTH], preferred_element_type=F32)
    for s in range(ATTN_WIDTH // LANES):
        cols = slice(s * LANES, (s + 1) * LANES)
        q_ref[:, cols] = (_rope_slab(q[:, cols], cos, sin) * ATTN_SCALE).astype(BF16)

    kv = jnp.dot(h, w_ref[:, ATTN_WIDTH:ATTN_WIDTH + 2 * KV_WIDTH], preferred_element_type=F32)
    for s in range(KV_WIDTH // LANES):
        cols = slice(s * LANES, (s + 1) * LANES)
        k = _rope_slab(kv[:, cols], cos, sin)
        kf_ref[:, cols] = k
        kv_ref[:, cols] = k.astype(BF16)
    v = kv[:, KV_WIDTH:]
    vf_ref[...] = v
    kv_ref[:, KV_WIDTH:] = v.astype(BF16)

    rest_ref[...] = jnp.dot(h, w_ref[:, ATTN_WIDTH + 2 * KV_WIDTH:], preferred_element_type=F32)


def _proj(x, mod, cos_tab, sin_tab, w_in, *, layer, row_of_tile, tm, n_ctx_tokens, lat_seq):
    t, d = x.shape
    l = layer
    n_ctx_tiles = n_ctx_tokens // tm
    tiles_per_seq = lat_seq // tm

    def rope_idx(i):
        return (jnp.where(i < n_ctx_tiles, 0, 1 + jnp.maximum(i - n_ctx_tiles, 0) % tiles_per_seq), 0)

    def rows(width):
        return pl.BlockSpec((tm, width), lambda i: (i, 0))

    return pl.pallas_call(
        _proj_kernel,
        grid=(t // tm,),
        in_specs=[
            rows(d),
            pl.BlockSpec((None, None, N_MOD, d), lambda i: (l, row_of_tile(i, tm), 0, 0)),
            pl.BlockSpec((tm, LANES), rope_idx),
            pl.BlockSpec((tm, LANES), rope_idx),
            pl.BlockSpec((None, d, IN_WIDTH), lambda i: (l, 0, 0), pipeline_mode=pl.Buffered(1)),
        ],
        out_specs=[rows(ATTN_WIDTH), rows(2 * KV_WIDTH), rows(KV_WIDTH), rows(KV_WIDTH), rows(REST_WIDTH)],
        out_shape=[
            jax.ShapeDtypeStruct((t, ATTN_WIDTH), BF16),
            jax.ShapeDtypeStruct((t, 2 * KV_WIDTH), BF16),
            jax.ShapeDtypeStruct((t, KV_WIDTH), F32),
            jax.ShapeDtypeStruct((t, KV_WIDTH), F32),
            jax.ShapeDtypeStruct((t, REST_WIDTH), F32),
        ],
        compiler_params=_params("arbitrary"),
        name="in_proj",
    )(x, mod, cos_tab, sin_tab, w_in)


def _lane_replicate(x, g):
    kg = x[:, g * HEAD_DIM:(g + 1) * HEAD_DIM].astype(BF16)
    return jnp.concatenate([kg] * GROUP, axis=1)


def _attend(sink_ref, q_ref, o_ref, k_loc, v_loc, valid, past):
    lane = lax.broadcasted_iota(jnp.int32, (BLOCK, GROUP_WIDTH), 1)
    head_of_lane = lane // HEAD_DIM
    contract_last = (((1,), (1,)), ((), ()))
    for g in range(N_KV_HEADS):
        cols = slice(g * GROUP_WIDTH, (g + 1) * GROUP_WIDTH)
        qg = q_ref[:, cols]
        q_stack = jnp.concatenate(
            [jnp.where(head_of_lane == h, qg, jnp.zeros_like(qg)) for h in range(GROUP)], axis=0)
        s_loc = lax.dot_general(q_stack, _lane_replicate(k_loc, g), contract_last, preferred_element_type=F32)
        if past is not None:
            s_past = lax.dot_general(q_stack, past[0][g], contract_last, preferred_element_type=F32)
        e_loc, e_past, inv = [], [], []
        for h in range(GROUP):
            rows = slice(h * BLOCK, (h + 1) * BLOCK)
            sink = sink_ref[g * GROUP + h]
            sl = jnp.where(valid, s_loc[rows], NEG_INF)
            m = jnp.maximum(jnp.max(sl, axis=-1, keepdims=True), sink)
            if past is not None:
                sp = s_past[rows]
                m = jnp.maximum(m, jnp.max(sp, axis=-1, keepdims=True))
            el = jnp.exp(sl - m)
            denom = jnp.sum(el, axis=-1, keepdims=True) + jnp.exp(sink - m)
            e_loc.append(el.astype(BF16))
            if past is not None:
                ep = jnp.exp(sp - m)
                denom = denom + jnp.sum(ep, axis=-1, keepdims=True)
                e_past.append(ep.astype(BF16))
            inv.append(1.0 / denom)
        pv = jnp.dot(jnp.concatenate(e_loc, axis=0), _lane_replicate(v_loc, g), preferred_element_type=F32)
        if past is not None:
            pv = pv + jnp.dot(jnp.concatenate(e_past, axis=0), past[1][g], preferred_element_type=F32)
        out = jnp.zeros((BLOCK, GROUP_WIDTH), F32)
        for h in range(GROUP):
            rows = slice(h * BLOCK, (h + 1) * BLOCK)
            out = jnp.where(head_of_lane == h, pv[rows] * inv[h], out)
        o_ref[:, cols] = out.astype(o_ref.dtype)


def _attn_kernel(sink_ref, q_ref, kvp_ref, kvc_ref, kvn_ref, ck_ref, cv_ref, o_ref,
                 kpast_ref, vpast_ref, *, n_ctx_blocks, ctx_blocks_per_seq, lat_blocks_per_seq):
    n = pl.program_id(0)
    is_ctx = n < n_ctx_blocks
    pos = jnp.where(is_ctx, n % ctx_blocks_per_seq, (n - n_ctx_blocks) % lat_blocks_per_seq)
    last = jnp.where(is_ctx, ctx_blocks_per_seq - 1, lat_blocks_per_seq - 1)

    qi = lax.broadcasted_iota(jnp.int32, (BLOCK, 3 * BLOCK), 0)
    ki = lax.broadcasted_iota(jnp.int32, (BLOCK, 3 * BLOCK), 1) - BLOCK
    k_lo = jnp.where(pos > 0, -BLOCK, 0)
    k_hi = jnp.where(pos < last, 2 * BLOCK - 1, BLOCK - 1)
    in_seq = (ki >= k_lo) & (ki <= k_hi)

    kv_loc = jnp.concatenate([kvp_ref[...], kvc_ref[...], kvn_ref[...]], axis=0)
    k_loc = kv_loc[:, :KV_WIDTH]
    v_loc = kv_loc[:, KV_WIDTH:]

    @pl.when(is_ctx)
    def _():
        _attend(sink_ref, q_ref, o_ref, k_loc, v_loc, in_seq, None)

    @pl.when(jnp.logical_and(jnp.logical_not(is_ctx), pos == 0))
    def _():
        for g in range(N_KV_HEADS):
            kpast_ref[g] = _lane_replicate(ck_ref[...], g)
            vpast_ref[g] = _lane_replicate(cv_ref[...], g)

    @pl.when(jnp.logical_not(is_ctx))
    def _():
        band = in_seq & (jnp.abs(qi - ki) <= BLOCK)
        _attend(sink_ref, q_ref, o_ref, k_loc, v_loc, band, (kpast_ref, vpast_ref))


def _attention(q, kv, sink, cache_k, cache_v, *, layer, n_ctx_tokens, ctx_seq, lat_seq):
    t = q.shape[0]
    nblk = t // BLOCK
    n_ctx_blocks = n_ctx_tokens // BLOCK
    cps = ctx_seq // BLOCK
    lps = lat_seq // BLOCK
    past = cache_k.shape[2]
    l = layer

    def seq_pos(n):
        return jnp.where(n < n_ctx_blocks, n % cps, (n - n_ctx_blocks) % lps)

    def prev_blk(n):
        return (n - (seq_pos(n) > 0).astype(jnp.int32), 0)

    def next_blk(n):
        last = jnp.where(n < n_ctx_blocks, cps - 1, lps - 1)
        return (n + (seq_pos(n) < last).astype(jnp.int32), 0)

    def past_idx(n):
        return (jnp.maximum(n - n_ctx_blocks, 0) // lps, l, 0, 0)

    same = lambda n: (n, 0)
    kv_spec = lambda idx: pl.BlockSpec((BLOCK, 2 * KV_WIDTH), idx)
    return pl.pallas_call(
        functools.partial(_attn_kernel, n_ctx_blocks=n_ctx_blocks, ctx_blocks_per_seq=cps,
                          lat_blocks_per_seq=lps),
        grid=(nblk,),
        in_specs=[
            pl.BlockSpec(memory_space=pltpu.SMEM),
            pl.BlockSpec((BLOCK, ATTN_WIDTH), same),
            kv_spec(prev_blk), kv_spec(same), kv_spec(next_blk),
            pl.BlockSpec((None, None, past, KV_WIDTH), past_idx),
            pl.BlockSpec((None, None, past, KV_WIDTH), past_idx),
        ],
        out_specs=pl.BlockSpec((BLOCK, ATTN_WIDTH), same),
        out_shape=jax.ShapeDtypeStruct((t, ATTN_WIDTH), BF16),
        scratch_shapes=[pltpu.VMEM((N_KV_HEADS, past, GROUP_WIDTH), BF16),
                        pltpu.VMEM((N_KV_HEADS, past, GROUP_WIDTH), BF16)],
        compiler_params=_params("arbitrary"),
        name="attention",
    )(sink, q, kv, kv, kv, cache_k, cache_v)


def _mix_kernel(x_ref, mod_ref, lng_ref, lnb_ref, attn_ref, su_ref, sv_ref, cx_ref, cb_ref, cc_ref,
                cxp_ref, cxn_ref, ccp_ref, ccn_ref, gain_ref, ws_ref, sb_ref, cw_ref, wo_ref, o_ref,
                *, alpha, tm, ctx_seq, n_ctx_tiles, lat_tiles_per_seq):
    i = pl.program_id(0)
    is_ctx = i < n_ctx_tiles
    ctx_tiles_per_seq = max(ctx_seq // tm, 1)
    pos = jnp.where(is_ctx, i % ctx_tiles_per_seq, (i - n_ctx_tiles) % lat_tiles_per_seq)
    last = jnp.where(is_ctx, ctx_tiles_per_seq - 1, lat_tiles_per_seq - 1)

    hp = cc_ref[...] * cx_ref[...]
    hp_prev = jnp.where(pos > 0, ccp_ref[SUBLANES - 1:SUBLANES, :] * cxp_ref[SUBLANES - 1:SUBLANES, :], 0.0)
    hp_next = jnp.where(pos < last, ccn_ref[0:1, :] * cxn_ref[0:1, :], 0.0)
    row = lax.broadcasted_iota(jnp.int32, hp.shape, 0)
    if ctx_seq < tm:
        row = jnp.where(is_ctx, row & (ctx_seq - 1), row)
    last_row = jnp.where(is_ctx, min(ctx_seq, tm) - 1, tm - 1)
    up = jnp.where(row == 0, hp_prev, pltpu.roll(hp, 1, 0))
    dn = jnp.where(row == last_row, hp_next, pltpu.roll(hp, tm - 1, 0))
    y = cw_ref[0:1, :] * up + cw_ref[1:2, :] * hp + cw_ref[2:3, :] * dn
    conv = (cb_ref[...] * y).astype(BF16)

    gate = mod_ref[5:6, :]
    g = lng_ref[1:2, :]
    b = lnb_ref[1:2, :]
    for rows in _row_chunks(tm, CHUNK):
        u = jax.nn.gelu(su_ref[rows, :])
        v = jax.nn.gelu(sv_ref[rows, :])
        mu = jnp.mean(v, axis=-1, keepdims=True)
        dv = v - mu
        var = jnp.mean(dv * dv, axis=-1, keepdims=True)
        vn = (dv * lax.rsqrt(var + LN_EPS) * gain_ref[...]).astype(BF16)
        parts = [attn_ref[rows, :]]
        for h in range(SGU_HEADS):
            cols = slice(h * SGU_HEAD_DIM, (h + 1) * SGU_HEAD_DIM)
            mixed = jnp.dot(ws_ref[h], vn[:, cols], preferred_element_type=F32) + sb_ref[:, cols]
            parts.append((u[:, cols] * mixed).astype(BF16))
        parts.append(conv[rows, :])
        res = jnp.dot(jnp.concatenate(parts, axis=1), wo_ref[...], preferred_element_type=F32)
        o_ref[rows, :] = _layer_norm_rows(alpha * x_ref[rows, :] + gate * res, g, b)


def _mix(x, mod, ln_g, ln_b, attn, rest, gain, ws, sb_full, conv_w, w_out, *, layer, alpha, row_of_tile,
         n_ctx_tokens, ctx_seq, lat_seq):
    t, d = x.shape
    tm = _pick_tile(min(n_ctx_tokens, lat_seq), 512)
    l = layer
    halo_per_tile = tm // SUBLANES
    n_halo_blocks = t // SUBLANES
    su_c, sv_c, cx_c, cb_c, cc_c = range(5)
    assert lat_seq % tm == 0 and (ctx_seq % tm == 0 or (tm % ctx_seq == 0 and ctx_seq & (ctx_seq - 1) == 0))

    def col(c):
        return pl.BlockSpec((tm, 512), lambda i: (i, c))

    def halo_prev(c):
        return pl.BlockSpec((SUBLANES, 512), lambda i: (jnp.maximum(i * halo_per_tile - 1, 0), c))

    def halo_next(c):
        return pl.BlockSpec((SUBLANES, 512),
                            lambda i: (jnp.minimum((i + 1) * halo_per_tile, n_halo_blocks - 1), c))

    return pl.pallas_call(
        functools.partial(_mix_kernel, alpha=alpha, tm=tm, ctx_seq=ctx_seq, n_ctx_tiles=n_ctx_tokens // tm,
                          lat_tiles_per_seq=lat_seq // tm),
        grid=(t // tm,),
        in_specs=[
            pl.BlockSpec((tm, d), lambda i: (i, 0)),
            pl.BlockSpec((None, None, N_MOD, d), lambda i: (l, row_of_tile(i, tm), 0, 0)),
            pl.BlockSpec((None, 3, d), lambda i: (l, 0, 0)),
            pl.BlockSpec((None, 3, d), lambda i: (l, 0, 0)),
            pl.BlockSpec((tm, ATTN_WIDTH), lambda i: (i, 0)),
            col(su_c), col(sv_c), col(cx_c), col(cb_c), col(cc_c),
            halo_prev(cx_c), halo_next(cx_c), halo_prev(cc_c), halo_next(cc_c),
            pl.BlockSpec((None, 1, SGU_WIDTH), lambda i: (l, 0, 0)),
            pl.BlockSpec((None, SGU_HEADS, CHUNK, CHUNK), lambda i: (l, 0, 0, 0)),
            pl.BlockSpec((None, CHUNK, SGU_WIDTH), lambda i: (l, 0, 0)),
            pl.BlockSpec((None, 3, CONV_WIDTH), lambda i: (l, 0, 0)),
            pl.BlockSpec((None, MIX_WIDTH, d), lambda i: (l, 0, 0), pipeline_mode=pl.Buffered(1)),
        ],
        out_specs=pl.BlockSpec((tm, d), lambda i: (i, 0)),
        out_shape=jax.ShapeDtypeStruct((t, d), F32),
        compiler_params=_params("arbitrary"),
        name="mix_out",
    )(x, mod, ln_g, ln_b, attn, rest, rest, rest, rest, rest, rest, rest, rest, rest,
      gain, ws, sb_full, conv_w, w_out)


def _rope_tables(lat_seq, identity_rows):
    pos = jnp.arange(lat_seq)
    row = (pos // GRID_W).astype(F32)
    col = (pos % GRID_W).astype(F32)
    n_freq = HEAD_DIM // 4
    inv = ROPE_BASE ** (-jnp.arange(n_freq, dtype=F32) / n_freq)
    ang = jnp.concatenate([row[:, None] * inv, col[:, None] * inv], axis=-1)
    cos = jnp.cos(ang)
    sin = jnp.sin(ang)
    cos_pairs = jnp.repeat(cos, 2, axis=-1)
    sin_pairs = jnp.stack([-sin, sin], axis=-1).reshape(lat_seq, HEAD_DIM)
    reps = LANES // HEAD_DIM
    cos_tab = jnp.concatenate([jnp.ones((identity_rows, LANES), F32), jnp.tile(cos_pairs, (1, reps))], axis=0)
    sin_tab = jnp.concatenate([jnp.zeros((identity_rows, LANES), F32), jnp.tile(sin_pairs, (1, reps))], axis=0)
    return cos_tab, sin_tab


def kernel(x_prompt, x_sample, cache_k, cache_v, c, c_ctx, ada_w, ada_b, ffn1_wi, ffn1_wo, ffn2_wi, ffn2_wo,
           w_in, w_out, attn_sink, sgu_gain, sgu_ws, sgu_b, conv_w, ln_g, ln_b):
    batch, ctx_seq, d = x_prompt.shape
    lat_batch, lat_seq, _ = x_sample.shape
    depth = ada_w.shape[0]
    past = cache_k.shape[2]
    n_ctx_tokens = batch * ctx_seq
    assert lat_batch + 1 <= MOD_ROWS
    assert ctx_seq % BLOCK == 0 and lat_seq % BLOCK == 0 and lat_seq % GRID_W == 0

    alpha = (2 * depth) ** 0.25

    def row_of_tile(i, tm):
        first_lat = n_ctx_tokens // tm
        return jnp.where(i < first_lat, 0, 1 + jnp.maximum(i - first_lat, 0) // (lat_seq // tm))

    cv = jnp.zeros((MOD_ROWS, d), F32).at[0].set(c_ctx).at[1:1 + lat_batch].set(c)
    mod = _modulation(cv, ada_w, ada_b)

    wi1, wo1 = ffn1_wi.astype(BF16), ffn1_wo.astype(BF16)
    wi2, wo2 = ffn2_wi.astype(BF16), ffn2_wo.astype(BF16)
    w_in_b, w_out_b = w_in.astype(BF16), w_out.astype(BF16)
    ws_b = sgu_ws.astype(BF16)
    sb_full = jnp.repeat(jnp.swapaxes(sgu_b, 1, 2), SGU_HEAD_DIM, axis=2)
    gain = sgu_gain.reshape(depth, 1, SGU_WIDTH)
    ck = cache_k.reshape(lat_batch, depth, past, KV_WIDTH)
    cvv = cache_v.reshape(lat_batch, depth, past, KV_WIDTH)

    tm = _pick_tile(min(n_ctx_tokens, lat_seq), 512)
    n_ctx_tiles = n_ctx_tokens // tm
    n_lat_tiles = lat_batch * lat_seq // tm
    n_all_tiles = n_ctx_tiles + n_lat_tiles
    cos_tab, sin_tab = _rope_tables(lat_seq, tm)
    common = dict(row_of_tile=row_of_tile)
    ffn_common = dict(alpha=alpha, tm=tm, **common)
    ks_new, vs_new = [], []
    for l in range(depth):
        if l == 0:
            x = _ffn(x_prompt.reshape(n_ctx_tokens, d), mod, ln_g, ln_b, wi1, wo1, layer=l, piece=0,
                     n_tiles=n_ctx_tiles, out_tiles=n_all_tiles, **ffn_common)
            x = _ffn(x_sample.reshape(lat_batch * lat_seq, d), mod, ln_g, ln_b, wi1, wo1, layer=l, piece=0,
                     first_tile=n_ctx_tiles, n_tiles=n_lat_tiles, x_first_tile=n_ctx_tiles,
                     out_tiles=n_all_tiles, prev=x, **ffn_common)
        else:
            x = _ffn(x, mod, ln_g, ln_b, wi1, wo1, layer=l, piece=0, **ffn_common)
        q, kv, kf, vf, rest = _proj(x, mod, cos_tab, sin_tab, w_in_b, layer=l, tm=tm,
                                    n_ctx_tokens=n_ctx_tokens, lat_seq=lat_seq, **common)
        attn = _attention(q, kv, attn_sink[l], ck, cvv, layer=l, n_ctx_tokens=n_ctx_tokens,
                          ctx_seq=ctx_seq, lat_seq=lat_seq)
        x = _mix(x, mod, ln_g, ln_b, attn, rest, gain, ws_b, sb_full, conv_w, w_out_b, layer=l, alpha=alpha,
                 n_ctx_tokens=n_ctx_tokens, ctx_seq=ctx_seq, lat_seq=lat_seq, **common)
        if l < depth - 1:
            x = _ffn(x, mod, ln_g, ln_b, wi2, wo2, layer=l, piece=2, **ffn_common)
        ks_new.append(kf[:n_ctx_tokens].reshape(batch, ctx_seq, N_KV_HEADS, HEAD_DIM))
        vs_new.append(vf[:n_ctx_tokens].reshape(batch, ctx_seq, N_KV_HEADS, HEAD_DIM))

    last = dict(layer=depth - 1, piece=2, **ffn_common)
    y_prompt = _ffn(x, mod, ln_g, ln_b, wi2, wo2, n_tiles=n_ctx_tiles, **last)
    y_sample = _ffn(x, mod, ln_g, ln_b, wi2, wo2, first_tile=n_ctx_tiles, n_tiles=n_lat_tiles,
                    out_first_tile=n_ctx_tiles, **last)
    y_prompt = y_prompt.reshape(batch, ctx_seq, d)
    y_sample = y_sample.reshape(lat_batch, lat_seq, d)
    return (y_prompt, y_sample, jnp.stack(ks_new, axis=1), jnp.stack(vs_new, axis=1))
```

```python
import functools

import jax
import jax.numpy as jnp
from jax import lax
from jax.experimental import pallas as pl
from jax.experimental.pallas import tpu as pltpu

GRID_W = 64
BLOCK = 128
N_HEADS = 16
N_KV_HEADS = 4
GROUP = N_HEADS // N_KV_HEADS
HEAD_DIM = 64
ATTN_WIDTH = N_HEADS * HEAD_DIM
KV_WIDTH = N_KV_HEADS * HEAD_DIM
GROUP_WIDTH = GROUP * HEAD_DIM
ATTN_SCALE = HEAD_DIM ** -0.5
ROPE_BASE = 10000.0
SGU_HEADS = 4
SGU_HEAD_DIM = 128
SGU_WIDTH = SGU_HEADS * SGU_HEAD_DIM
CHUNK = 128
CONV_WIDTH = 512
MIX_WIDTH = ATTN_WIDTH + SGU_WIDTH + CONV_WIDTH
IN_WIDTH = ATTN_WIDTH + 2 * KV_WIDTH + 2 * SGU_WIDTH + 3 * CONV_WIDTH
REST_WIDTH = 2 * SGU_WIDTH + 3 * CONV_WIDTH
N_MOD = 9
LN_EPS = 1e-5
NEG_INF = -1e30

LANES = 128
SUBLANES = 8
MXU_WIDTH = 256
DOWN_SLAB = 512
OUT_ROWS = 256
MOD_ROWS = 16
VMEM_LIMIT_BYTES = 56 * 1024 * 1024

BF16 = jnp.bfloat16
F32 = jnp.float32


def _params(*sem):
    return pltpu.CompilerParams(dimension_semantics=sem, vmem_limit_bytes=VMEM_LIMIT_BYTES)


def _pick_tile(n, pref):
    t = min(n, pref)
    while n % t:
        t //= 2
    return t


def _row_chunks(n, size):
    return [slice(r, min(r + size, n)) for r in range(0, n, size)]


def _layer_norm_rows(y, g, b):
    mu = jnp.mean(y, axis=-1, keepdims=True)
    d = y - mu
    var = jnp.mean(d * d, axis=-1, keepdims=True)
    return d * lax.rsqrt(var + LN_EPS) * g + b


def _mod_kernel(cv_ref, w_ref, b_ref, o_ref):
    a = jax.nn.silu(cv_ref[...]).astype(BF16)
    o_ref[...] = jnp.dot(a, w_ref[...].astype(BF16), preferred_element_type=F32) + b_ref[...]


def _modulation(cv, ada_w, ada_b):
    depth, d, nd = ada_w.shape
    tn = _pick_tile(nd, 1024)
    out = pl.pallas_call(
        _mod_kernel,
        grid=(depth, nd // tn),
        in_specs=[
            pl.BlockSpec((MOD_ROWS, d), lambda l, n: (0, 0)),
            pl.BlockSpec((None, d, tn), lambda l, n: (l, 0, n)),
            pl.BlockSpec((None, 1, tn), lambda l, n: (l, 0, n)),
        ],
        out_specs=pl.BlockSpec((None, MOD_ROWS, tn), lambda l, n: (l, 0, n)),
        out_shape=jax.ShapeDtypeStruct((depth, MOD_ROWS, nd), F32),
        compiler_params=_params("arbitrary", "arbitrary"),
        name="modulation",
    )(cv, ada_w, ada_b.reshape(depth, 1, nd))
    return out.reshape(depth, MOD_ROWS, N_MOD, d)


def _ffn_kernel(xa_hbm, xb_hbm, mod_ref, lng_ref, lnb_ref, wig_ref, wiu_ref, wo_ref, o_ref, x_buf, h_ref, x_sem,
                *, piece, alpha, nj, chunk, n_tiles, x_tile0, xa_tiles):
    i = pl.program_id(0)
    j = pl.program_id(1)
    tm, d = o_ref.shape
    gate = 0.5 * mod_ref[3 * piece + 2:3 * piece + 3, :]

    def x_copy(tile, act):
        def run(src, t):
            start = pl.multiple_of(t * tm, tm)
            act(pltpu.make_async_copy(src.at[pl.ds(start, tm), :], x_buf, x_sem))

        tile = tile + x_tile0
        pl.when(tile < xa_tiles)(lambda: run(xa_hbm, tile))
        pl.when(tile >= xa_tiles)(lambda: run(xb_hbm, tile - xa_tiles))

    def start(cp):
        cp.start()

    def wait(cp):
        cp.wait()

    def modulated(rows):
        sh = mod_ref[3 * piece:3 * piece + 1, :]
        sc = mod_ref[3 * piece + 1:3 * piece + 2, :]
        return (x_buf[rows, :] * (1.0 + sc) + sh).astype(BF16)

    def up(h):
        tf = wig_ref.shape[1]
        parts = []
        for c in range(0, tf, MXU_WIDTH):
            cols = slice(c, min(c + MXU_WIDTH, tf))
            g = jnp.dot(h, wig_ref[:, cols], preferred_element_type=F32)
            u = jnp.dot(h, wiu_ref[:, cols], preferred_element_type=F32)
            parts.append((jax.nn.silu(g) * u).astype(BF16))
        return jnp.concatenate(parts, axis=1)

    def gated_down(a, cols):
        return gate[:, cols] * jnp.dot(a, wo_ref[:, cols], preferred_element_type=F32)

    def layer_norm(y):
        return _layer_norm_rows(y, lng_ref[piece:piece + 1, :], lnb_ref[piece:piece + 1, :])

    col_slabs = _row_chunks(d, DOWN_SLAB)
    everything = slice(0, d)

    @pl.when(jnp.logical_and(i == 0, j == 0))
    def _():
        x_copy(i, start)

    @pl.when(j == 0)
    def _():
        x_copy(i, wait)

    if nj == 1:
        for rows in _row_chunks(tm, chunk):
            y = alpha * x_buf[rows, :] + gated_down(up(modulated(rows)), everything)
            o_ref[rows, :] = layer_norm(y)
    else:
        @pl.when(j == 0)
        def _():
            h = modulated(slice(0, tm))
            h_ref[...] = h
            a = up(h)
            for cols in col_slabs:
                o_ref[:, cols] = alpha * x_buf[:, cols] + gated_down(a, cols)

        @pl.when(jnp.logical_and(j > 0, j < nj - 1))
        def _():
            a = up(h_ref[...])
            for cols in col_slabs:
                o_ref[:, cols] += gated_down(a, cols)

        @pl.when(j == nj - 1)
        def _():
            for rows in _row_chunks(tm, chunk):
                y = o_ref[rows, :] + gated_down(up(h_ref[rows, :]), everything)
                o_ref[rows, :] = layer_norm(y)

    @pl.when(jnp.logical_and(j == 0, i + 1 < n_tiles))
    def _():
        x_copy(i + 1, start)


def _ffn(xs, mod, ln_g, ln_b, wi, wo, *, layer, piece, alpha, row_of_tile, tm, first_tile=0, n_tiles=None):
    xa, xb = xs if isinstance(xs, tuple) else (xs, xs)
    d = xa.shape[1]
    f = wo.shape[1]
    tf = _pick_tile(f, 512)
    nj = f // tf
    l = layer
    xa_tiles = xa.shape[0] // tm
    if n_tiles is None:
        n_tiles = xa_tiles + (xb.shape[0] // tm if isinstance(xs, tuple) else 0) - first_tile
    if not isinstance(xs, tuple):
        assert first_tile + n_tiles <= xa_tiles
    return pl.pallas_call(
        functools.partial(_ffn_kernel, piece=piece, alpha=alpha, nj=nj, chunk=_pick_tile(tm, 256),
                          n_tiles=n_tiles, x_tile0=first_tile, xa_tiles=xa_tiles),
        grid=(n_tiles, nj),
        in_specs=[
            pl.BlockSpec(memory_space=pl.ANY),
            pl.BlockSpec(memory_space=pl.ANY),
            pl.BlockSpec((None, None, N_MOD, d), lambda i, j: (l, row_of_tile(i + first_tile, tm), 0, 0)),
            pl.BlockSpec((None, 3, d), lambda i, j: (l, 0, 0)),
            pl.BlockSpec((None, 3, d), lambda i, j: (l, 0, 0)),
            pl.BlockSpec((None, d, tf), lambda i, j: (l, 0, j)),
            pl.BlockSpec((None, d, tf), lambda i, j: (l, 0, j + nj)),
            pl.BlockSpec((None, tf, d), lambda i, j: (l, j, 0)),
        ],
        out_specs=pl.BlockSpec((tm, d), lambda i, j: (i, 0)),
        out_shape=jax.ShapeDtypeStruct((n_tiles * tm, d), F32),
        scratch_shapes=[pltpu.VMEM((tm, d), F32), pltpu.VMEM((tm, d), BF16), pltpu.SemaphoreType.DMA(())],
        compiler_params=_params("arbitrary", "arbitrary"),
        name=f"ffn{piece}",
    )(xa, xb, mod, ln_g, ln_b, wi, wi, wo)


def _rope_slab(xs, cos, sin_signed):
    lane = lax.broadcasted_iota(jnp.int32, xs.shape, 1)
    partner = jnp.where(lane % 2 == 0, pltpu.roll(xs, LANES - 1, 1), pltpu.roll(xs, 1, 1))
    return xs * cos + partner * sin_signed


def _proj_kernel(x_ref, mod_ref, cos_ref, sin_ref, w_ref, q_ref, kv_ref, kf_ref, vf_ref, rest_ref):
    sh = mod_ref[3:4, :]
    sc = mod_ref[4:5, :]
    h = (x_ref[...] * (1.0 + sc) + sh).astype(BF16)
    cos = cos_ref[...]
    sin = sin_ref[...]

    q = jnp.dot(h, w_ref[:, :ATTN_WIDTH], preferred_element_type=F32)
    for s in range(ATTN_WIDTH // LANES):
        cols = slice(s * LANES, (s + 1) * LANES)
        q_ref[:, cols] = (_rope_slab(q[:, cols], cos, sin) * ATTN_SCALE).astype(BF16)

    kv = jnp.dot(h, w_ref[:, ATTN_WIDTH:ATTN_WIDTH + 2 * KV_WIDTH], preferred_element_type=F32)
    for s in range(KV_WIDTH // LANES):
        cols = slice(s * LANES, (s + 1) * LANES)
        k = _rope_slab(kv[:, cols], cos, sin)
        kf_ref[:, cols] = k
        kv_ref[:, cols] = k.astype(BF16)
    v = kv[:, KV_WIDTH:]
    vf_ref[...] = v
    kv_ref[:, KV_WIDTH:] = v.astype(BF16)

    rest_ref[...] = jnp.dot(h, w_ref[:, ATTN_WIDTH + 2 * KV_WIDTH:], preferred_element_type=F32)


def _proj(x, mod, cos_tab, sin_tab, w_in, *, layer, row_of_tile, tm, n_ctx_tokens, lat_seq):
    t, d = x.shape
    l = layer
    n_ctx_tiles = n_ctx_tokens // tm
    tiles_per_seq = lat_seq // tm

    def rope_idx(i):
        return (jnp.where(i < n_ctx_tiles, 0, 1 + jnp.maximum(i - n_ctx_tiles, 0) % tiles_per_seq), 0)

    def rows(width):
        return pl.BlockSpec((tm, width), lambda i: (i, 0))

    return pl.pallas_call(
        _proj_kernel,
        grid=(t // tm,),
        in_specs=[
            rows(d),
            pl.BlockSpec((None, None, N_MOD, d), lambda i: (l, row_of_tile(i, tm), 0, 0)),
            pl.BlockSpec((tm, LANES), rope_idx),
            pl.BlockSpec((tm, LANES), rope_idx),
            pl.BlockSpec((None, d, IN_WIDTH), lambda i: (l, 0, 0), pipeline_mode=pl.Buffered(1)),
        ],
        out_specs=[rows(ATTN_WIDTH), rows(2 * KV_WIDTH), rows(KV_WIDTH), rows(KV_WIDTH), rows(REST_WIDTH)],
        out_shape=[
            jax.ShapeDtypeStruct((t, ATTN_WIDTH), BF16),
            jax.ShapeDtypeStruct((t, 2 * KV_WIDTH), BF16),
            jax.ShapeDtypeStruct((t, KV_WIDTH), F32),
            jax.ShapeDtypeStruct((t, KV_WIDTH), F32),
            jax.ShapeDtypeStruct((t, REST_WIDTH), F32),
        ],
        compiler_params=_params("arbitrary"),
        name="in_proj",
    )(x, mod, cos_tab, sin_tab, w_in)


def _lane_replicate(x, g):
    kg = x[:, g * HEAD_DIM:(g + 1) * HEAD_DIM].astype(BF16)
    return jnp.concatenate([kg] * GROUP, axis=1)


def _attend(sink_ref, q_ref, o_ref, k_loc, v_loc, valid, past):
    lane = lax.broadcasted_iota(jnp.int32, (BLOCK, GROUP_WIDTH), 1)
    head_of_lane = lane // HEAD_DIM
    contract_last = (((1,), (1,)), ((), ()))
    for g in range(N_KV_HEADS):
        cols = slice(g * GROUP_WIDTH, (g + 1) * GROUP_WIDTH)
        qg = q_ref[:, cols]
        q_stack = jnp.concatenate(
            [jnp.where(head_of_lane == h, qg, jnp.zeros_like(qg)) for h in range(GROUP)], axis=0)
        s_loc = lax.dot_general(q_stack, _lane_replicate(k_loc, g), contract_last, preferred_element_type=F32)
        if past is not None:
            s_past = lax.dot_general(q_stack, past[0][g], contract_last, preferred_element_type=F32)
        e_loc, e_past, inv = [], [], []
        for h in range(GROUP):
            rows = slice(h * BLOCK, (h + 1) * BLOCK)
            sink = sink_ref[g * GROUP + h]
            sl = jnp.where(valid, s_loc[rows], NEG_INF)
            m = jnp.maximum(jnp.max(sl, axis=-1, keepdims=True), sink)
            if past is not None:
                sp = s_past[rows]
                m = jnp.maximum(m, jnp.max(sp, axis=-1, keepdims=True))
            el = jnp.exp(sl - m)
            denom = jnp.sum(el, axis=-1, keepdims=True) + jnp.exp(sink - m)
            e_loc.append(el.astype(BF16))
            if past is not None:
                ep = jnp.exp(sp - m)
                denom = denom + jnp.sum(ep, axis=-1, keepdims=True)
                e_past.append(ep.astype(BF16))
            inv.append(1.0 / denom)
        pv = jnp.dot(jnp.concatenate(e_loc, axis=0), _lane_replicate(v_loc, g), preferred_element_type=F32)
        if past is not None:
            pv = pv + jnp.dot(jnp.concatenate(e_past, axis=0), past[1][g], preferred_element_type=F32)
        out = jnp.zeros((BLOCK, GROUP_WIDTH), F32)
        for h in range(GROUP):
            rows = slice(h * BLOCK, (h + 1) * BLOCK)
            out = jnp.where(head_of_lane == h, pv[rows] * inv[h], out)
        o_ref[:, cols] = out.astype(o_ref.dtype)


def _attn_kernel(sink_ref, q_ref, kvp_ref, kvc_ref, kvn_ref, ck_ref, cv_ref, o_ref,
                 kpast_ref, vpast_ref, *, n_ctx_blocks, ctx_blocks_per_seq, lat_blocks_per_seq):
    n = pl.program_id(0)
    is_ctx = n < n_ctx_blocks
    pos = jnp.where(is_ctx, n % ctx_blocks_per_seq, (n - n_ctx_blocks) % lat_blocks_per_seq)
    last = jnp.where(is_ctx, ctx_blocks_per_seq - 1, lat_blocks_per_seq - 1)

    qi = lax.broadcasted_iota(jnp.int32, (BLOCK, 3 * BLOCK), 0)
    ki = lax.broadcasted_iota(jnp.int32, (BLOCK, 3 * BLOCK), 1) - BLOCK
    k_lo = jnp.where(pos > 0, -BLOCK, 0)
    k_hi = jnp.where(pos < last, 2 * BLOCK - 1, BLOCK - 1)
    in_seq = (ki >= k_lo) & (ki <= k_hi)

    kv_loc = jnp.concatenate([kvp_ref[...], kvc_ref[...], kvn_ref[...]], axis=0)
    k_loc = kv_loc[:, :KV_WIDTH]
    v_loc = kv_loc[:, KV_WIDTH:]

    @pl.when(is_ctx)
    def _():
        _attend(sink_ref, q_ref, o_ref, k_loc, v_loc, in_seq, None)

    @pl.when(jnp.logical_and(jnp.logical_not(is_ctx), pos == 0))
    def _():
        for g in range(N_KV_HEADS):
            kpast_ref[g] = _lane_replicate(ck_ref[...], g)
            vpast_ref[g] = _lane_replicate(cv_ref[...], g)

    @pl.when(jnp.logical_not(is_ctx))
    def _():
        band = in_seq & (jnp.abs(qi - ki) <= BLOCK)
        _attend(sink_ref, q_ref, o_ref, k_loc, v_loc, band, (kpast_ref, vpast_ref))


def _attention(q, kv, sink, cache_k, cache_v, *, layer, n_ctx_tokens, ctx_seq, lat_seq):
    t = q.shape[0]
    nblk = t // BLOCK
    n_ctx_blocks = n_ctx_tokens // BLOCK
    cps = ctx_seq // BLOCK
    lps = lat_seq // BLOCK
    past = cache_k.shape[2]
    l = layer

    def seq_pos(n):
        return jnp.where(n < n_ctx_blocks, n % cps, (n - n_ctx_blocks) % lps)

    def prev_blk(n):
        return (n - (seq_pos(n) > 0).astype(jnp.int32), 0)

    def next_blk(n):
        last = jnp.where(n < n_ctx_blocks, cps - 1, lps - 1)
        return (n + (seq_pos(n) < last).astype(jnp.int32), 0)

    def past_idx(n):
        return (jnp.maximum(n - n_ctx_blocks, 0) // lps, l, 0, 0)

    same = lambda n: (n, 0)
    kv_spec = lambda idx: pl.BlockSpec((BLOCK, 2 * KV_WIDTH), idx)
    return pl.pallas_call(
        functools.partial(_attn_kernel, n_ctx_blocks=n_ctx_blocks, ctx_blocks_per_seq=cps,
                          lat_blocks_per_seq=lps),
        grid=(nblk,),
        in_specs=[
            pl.BlockSpec(memory_space=pltpu.SMEM),
            pl.BlockSpec((BLOCK, ATTN_WIDTH), same),
            kv_spec(prev_blk), kv_spec(same), kv_spec(next_blk),
            pl.BlockSpec((None, None, past, KV_WIDTH), past_idx),
            pl.BlockSpec((None, None, past, KV_WIDTH), past_idx),
        ],
        out_specs=pl.BlockSpec((BLOCK, ATTN_WIDTH), same),
        out_shape=jax.ShapeDtypeStruct((t, ATTN_WIDTH), BF16),
        scratch_shapes=[pltpu.VMEM((N_KV_HEADS, past, GROUP_WIDTH), BF16),
                        pltpu.VMEM((N_KV_HEADS, past, GROUP_WIDTH), BF16)],
        compiler_params=_params("arbitrary"),
        name="attention",
    )(sink, q, kv, kv, kv, cache_k, cache_v)


def _mix_kernel(x_ref, mod_ref, lng_ref, lnb_ref, attn_ref, su_ref, sv_ref, cx_ref, cb_ref, cc_ref,
                cxp_ref, cxn_ref, ccp_ref, ccn_ref, gain_ref, ws_ref, sb_ref, cw_ref, wo_ref, o_ref,
                *, alpha, tm, ctx_seq, n_ctx_tiles, lat_tiles_per_seq):
    i = pl.program_id(0)
    is_ctx = i < n_ctx_tiles
    ctx_tiles_per_seq = max(ctx_seq // tm, 1)
    pos = jnp.where(is_ctx, i % ctx_tiles_per_seq, (i - n_ctx_tiles) % lat_tiles_per_seq)
    last = jnp.where(is_ctx, ctx_tiles_per_seq - 1, lat_tiles_per_seq - 1)

    hp = cc_ref[...] * cx_ref[...]
    hp_prev = jnp.where(pos > 0, ccp_ref[SUBLANES - 1:SUBLANES, :] * cxp_ref[SUBLANES - 1:SUBLANES, :], 0.0)
    hp_next = jnp.where(pos < last, ccn_ref[0:1, :] * cxn_ref[0:1, :], 0.0)
    row = lax.broadcasted_iota(jnp.int32, hp.shape, 0)
    if ctx_seq < tm:
        row = jnp.where(is_ctx, row & (ctx_seq - 1), row)
    last_row = jnp.where(is_ctx, min(ctx_seq, tm) - 1, tm - 1)
    up = jnp.where(row == 0, hp_prev, pltpu.roll(hp, 1, 0))
    dn = jnp.where(row == last_row, hp_next, pltpu.roll(hp, tm - 1, 0))
    y = cw_ref[0:1, :] * up + cw_ref[1:2, :] * hp + cw_ref[2:3, :] * dn
    conv = (cb_ref[...] * y).astype(BF16)

    gate = mod_ref[5:6, :]
    g = lng_ref[1:2, :]
    b = lnb_ref[1:2, :]
    def gating(rows):
        u = jax.nn.gelu(su_ref[rows, :])
        v = jax.nn.gelu(sv_ref[rows, :])
        mu = jnp.mean(v, axis=-1, keepdims=True)
        dv = v - mu
        var = jnp.mean(dv * dv, axis=-1, keepdims=True)
        vn = (dv * lax.rsqrt(var + LN_EPS) * gain_ref[...]).astype(BF16)
        parts = []
        for h in range(SGU_HEADS):
            cols = slice(h * SGU_HEAD_DIM, (h + 1) * SGU_HEAD_DIM)
            mixed = jnp.dot(ws_ref[h], vn[:, cols], preferred_element_type=F32) + sb_ref[:, cols]
            parts.append((u[:, cols] * mixed).astype(BF16))
        return jnp.concatenate(parts, axis=1)

    for rows in _row_chunks(tm, OUT_ROWS):
        sgu = jnp.concatenate([gating(slice(c, c + CHUNK)) for c in range(rows.start, rows.stop, CHUNK)], axis=0)
        mix = jnp.concatenate([attn_ref[rows, :], sgu, conv[rows, :]], axis=1)
        res = jnp.dot(mix, wo_ref[...], preferred_element_type=F32)
        o_ref[rows, :] = _layer_norm_rows(alpha * x_ref[rows, :] + gate * res, g, b)


def _mix(x, mod, ln_g, ln_b, attn, rest, gain, ws, sb_full, conv_w, w_out, *, layer, alpha, row_of_tile,
         n_ctx_tokens, ctx_seq, lat_seq):
    t, d = x.shape
    tm = _pick_tile(min(n_ctx_tokens, lat_seq), 512)
    l = layer
    halo_per_tile = tm // SUBLANES
    n_halo_blocks = t // SUBLANES
    su_c, sv_c, cx_c, cb_c, cc_c = range(5)
    assert lat_seq % tm == 0 and (ctx_seq % tm == 0 or (tm % ctx_seq == 0 and ctx_seq & (ctx_seq - 1) == 0))

    def col(c):
        return pl.BlockSpec((tm, 512), lambda i: (i, c))

    def halo_prev(c):
        return pl.BlockSpec((SUBLANES, 512), lambda i: (jnp.maximum(i * halo_per_tile - 1, 0), c))

    def halo_next(c):
        return pl.BlockSpec((SUBLANES, 512),
                            lambda i: (jnp.minimum((i + 1) * halo_per_tile, n_halo_blocks - 1), c))

    return pl.pallas_call(
        functools.partial(_mix_kernel, alpha=alpha, tm=tm, ctx_seq=ctx_seq, n_ctx_tiles=n_ctx_tokens // tm,
                          lat_tiles_per_seq=lat_seq // tm),
        grid=(t // tm,),
        in_specs=[
            pl.BlockSpec((tm, d), lambda i: (i, 0)),
            pl.BlockSpec((None, None, N_MOD, d), lambda i: (l, row_of_tile(i, tm), 0, 0)),
            pl.BlockSpec((None, 3, d), lambda i: (l, 0, 0)),
            pl.BlockSpec((None, 3, d), lambda i: (l, 0, 0)),
            pl.BlockSpec((tm, ATTN_WIDTH), lambda i: (i, 0)),
            col(su_c), col(sv_c), col(cx_c), col(cb_c), col(cc_c),
            halo_prev(cx_c), halo_next(cx_c), halo_prev(cc_c), halo_next(cc_c),
            pl.BlockSpec((None, 1, SGU_WIDTH), lambda i: (l, 0, 0)),
            pl.BlockSpec((None, SGU_HEADS, CHUNK, CHUNK), lambda i: (l, 0, 0, 0)),
            pl.BlockSpec((None, CHUNK, SGU_WIDTH), lambda i: (l, 0, 0)),
            pl.BlockSpec((None, 3, CONV_WIDTH), lambda i: (l, 0, 0)),
            pl.BlockSpec((None, MIX_WIDTH, d), lambda i: (l, 0, 0), pipeline_mode=pl.Buffered(1)),
        ],
        out_specs=pl.BlockSpec((tm, d), lambda i: (i, 0)),
        out_shape=jax.ShapeDtypeStruct((t, d), F32),
        compiler_params=_params("arbitrary"),
        name="mix_out",
    )(x, mod, ln_g, ln_b, attn, rest, rest, rest, rest, rest, rest, rest, rest, rest,
      gain, ws, sb_full, conv_w, w_out)


def _rope_tables(lat_seq, identity_rows):
    pos = jnp.arange(lat_seq)
    row = (pos // GRID_W).astype(F32)
    col = (pos % GRID_W).astype(F32)
    n_freq = HEAD_DIM // 4
    inv = ROPE_BASE ** (-jnp.arange(n_freq, dtype=F32) / n_freq)
    ang = jnp.concatenate([row[:, None] * inv, col[:, None] * inv], axis=-1)
    cos = jnp.cos(ang)
    sin = jnp.sin(ang)
    cos_pairs = jnp.repeat(cos, 2, axis=-1)
    sin_pairs = jnp.stack([-sin, sin], axis=-1).reshape(lat_seq, HEAD_DIM)
    reps = LANES // HEAD_DIM
    cos_tab = jnp.concatenate([jnp.ones((identity_rows, LANES), F32), jnp.tile(cos_pairs, (1, reps))], axis=0)
    sin_tab = jnp.concatenate([jnp.zeros((identity_rows, LANES), F32), jnp.tile(sin_pairs, (1, reps))], axis=0)
    return cos_tab, sin_tab


def kernel(x_prompt, x_sample, cache_k, cache_v, c, c_ctx, ada_w, ada_b, ffn1_wi, ffn1_wo, ffn2_wi, ffn2_wo,
           w_in, w_out, attn_sink, sgu_gain, sgu_ws, sgu_b, conv_w, ln_g, ln_b):
    batch, ctx_seq, d = x_prompt.shape
    lat_batch, lat_seq, _ = x_sample.shape
    depth = ada_w.shape[0]
    past = cache_k.shape[2]
    n_ctx_tokens = batch * ctx_seq
    assert lat_batch + 1 <= MOD_ROWS
    assert ctx_seq % BLOCK == 0 and lat_seq % BLOCK == 0 and lat_seq % GRID_W == 0

    alpha = (2 * depth) ** 0.25

    def row_of_tile(i, tm):
        first_lat = n_ctx_tokens // tm
        return jnp.where(i < first_lat, 0, 1 + jnp.maximum(i - first_lat, 0) // (lat_seq // tm))

    cv = jnp.zeros((MOD_ROWS, d), F32).at[0].set(c_ctx).at[1:1 + lat_batch].set(c)
    mod = _modulation(cv, ada_w, ada_b)

    wi1, wo1 = ffn1_wi.astype(BF16), ffn1_wo.astype(BF16)
    wi2, wo2 = ffn2_wi.astype(BF16), ffn2_wo.astype(BF16)
    w_in_b, w_out_b = w_in.astype(BF16), w_out.astype(BF16)
    ws_b = sgu_ws.astype(BF16)
    sb_full = jnp.repeat(jnp.swapaxes(sgu_b, 1, 2), SGU_HEAD_DIM, axis=2)
    gain = sgu_gain.reshape(depth, 1, SGU_WIDTH)
    ck = cache_k.reshape(lat_batch, depth, past, KV_WIDTH)
    cvv = cache_v.reshape(lat_batch, depth, past, KV_WIDTH)

    tm = _pick_tile(min(n_ctx_tokens, lat_seq), 512)
    tm_ffn = _pick_tile(min(n_ctx_tokens, lat_seq), 1024)
    n_ctx_tiles = n_ctx_tokens // tm_ffn
    n_lat_tiles = lat_batch * lat_seq // tm_ffn
    cos_tab, sin_tab = _rope_tables(lat_seq, tm)
    x = (x_prompt.reshape(n_ctx_tokens, d), x_sample.reshape(lat_batch * lat_seq, d))
    common = dict(row_of_tile=row_of_tile)
    ffn_common = dict(alpha=alpha, tm=tm_ffn, **common)
    ks_new, vs_new = [], []
    for l in range(depth):
        x = _ffn(x, mod, ln_g, ln_b, wi1, wo1, layer=l, piece=0, **ffn_common)
        q, kv, kf, vf, rest = _proj(x, mod, cos_tab, sin_tab, w_in_b, layer=l, tm=tm,
                                    n_ctx_tokens=n_ctx_tokens, lat_seq=lat_seq, **common)
        attn = _attention(q, kv, attn_sink[l], ck, cvv, layer=l, n_ctx_tokens=n_ctx_tokens,
                          ctx_seq=ctx_seq, lat_seq=lat_seq)
        x = _mix(x, mod, ln_g, ln_b, attn, rest, gain, ws_b, sb_full, conv_w, w_out_b, layer=l, alpha=alpha,
                 n_ctx_tokens=n_ctx_tokens, ctx_seq=ctx_seq, lat_seq=lat_seq, **common)
        if l < depth - 1:
            x = _ffn(x, mod, ln_g, ln_b, wi2, wo2, layer=l, piece=2, **ffn_common)
        ks_new.append(kf[:n_ctx_tokens].reshape(batch, ctx_seq, N_KV_HEADS, HEAD_DIM))
        vs_new.append(vf[:n_ctx_tokens].reshape(batch, ctx_seq, N_KV_HEADS, HEAD_DIM))

    last = dict(layer=depth - 1, piece=2, **ffn_common)
    y_prompt = _ffn(x, mod, ln_g, ln_b, wi2, wo2, n_tiles=n_ctx_tiles, **last)
    y_sample = _ffn(x, mod, ln_g, ln_b, wi2, wo2, first_tile=n_ctx_tiles, n_tiles=n_lat_tiles, **last)
    y_prompt = y_prompt.reshape(batch, ctx_seq, d)
    y_sample = y_sample.reshape(lat_batch, lat_seq, d)
    return (y_prompt, y_sample, jnp.stack(ks_new, axis=1), jnp.stack(vs_new, axis=1))
```

```python
import functools

import jax
import jax.numpy as jnp
from jax import lax
from jax.experimental import pallas as pl
from jax.experimental.pallas import tpu as pltpu

GRID_W = 64
BLOCK = 128
N_HEADS = 16
N_KV_HEADS = 4
GROUP = N_HEADS // N_KV_HEADS
HEAD_DIM = 64
ATTN_WIDTH = N_HEADS * HEAD_DIM
KV_WIDTH = N_KV_HEADS * HEAD_DIM
GROUP_WIDTH = GROUP * HEAD_DIM
ATTN_SCALE = HEAD_DIM ** -0.5
ROPE_BASE = 10000.0
SGU_HEADS = 4
SGU_HEAD_DIM = 128
SGU_WIDTH = SGU_HEADS * SGU_HEAD_DIM
CHUNK = 128
CONV_WIDTH = 512
MIX_WIDTH = ATTN_WIDTH + SGU_WIDTH + CONV_WIDTH
IN_WIDTH = ATTN_WIDTH + 2 * KV_WIDTH + 2 * SGU_WIDTH + 3 * CONV_WIDTH
REST_WIDTH = 2 * SGU_WIDTH + 3 * CONV_WIDTH
N_MOD = 9
LN_EPS = 1e-5
NEG_INF = -1e30

LANES = 128
SUBLANES = 8
MXU_WIDTH = 256
DOWN_SLAB = 512
OUT_ROWS = 256
MOD_ROWS = 16
VMEM_LIMIT_BYTES = 56 * 1024 * 1024

BF16 = jnp.bfloat16
F32 = jnp.float32


def _params(*sem):
    return pltpu.CompilerParams(dimension_semantics=sem, vmem_limit_bytes=VMEM_LIMIT_BYTES)


def _pick_tile(n, pref):
    t = min(n, pref)
    while n % t:
        t //= 2
    return t


def _row_chunks(n, size):
    return [slice(r, min(r + size, n)) for r in range(0, n, size)]


def _layer_norm_rows(y, g, b):
    mu = jnp.mean(y, axis=-1, keepdims=True)
    d = y - mu
    var = jnp.mean(d * d, axis=-1, keepdims=True)
    return d * lax.rsqrt(var + LN_EPS) * g + b


def _mod_kernel(cv_ref, w_ref, b_ref, o_ref):
    a = jax.nn.silu(cv_ref[...]).astype(BF16)
    o_ref[...] = jnp.dot(a, w_ref[...].astype(BF16), preferred_element_type=F32) + b_ref[...]


def _modulation(cv, ada_w, ada_b):
    depth, d, nd = ada_w.shape
    tn = _pick_tile(nd, 1024)
    out = pl.pallas_call(
        _mod_kernel,
        grid=(depth, nd // tn),
        in_specs=[
            pl.BlockSpec((MOD_ROWS, d), lambda l, n: (0, 0)),
            pl.BlockSpec((None, d, tn), lambda l, n: (l, 0, n)),
            pl.BlockSpec((None, 1, tn), lambda l, n: (l, 0, n)),
        ],
        out_specs=pl.BlockSpec((None, MOD_ROWS, tn), lambda l, n: (l, 0, n)),
        out_shape=jax.ShapeDtypeStruct((depth, MOD_ROWS, nd), F32),
        compiler_params=_params("arbitrary", "arbitrary"),
        name="modulation",
    )(cv, ada_w, ada_b.reshape(depth, 1, nd))
    return out.reshape(depth, MOD_ROWS, N_MOD, d)


def _ffn_kernel(xa_hbm, xb_hbm, mod_ref, lng_ref, lnb_ref, wi_ref, wo_ref, o_ref, x_buf, h_ref, x_sem,
                *, piece, alpha, nj, chunk, n_tiles, x_tile0, xa_tiles):
    i = pl.program_id(0)
    j = pl.program_id(1)
    tm, d = o_ref.shape
    gate = 0.5 * mod_ref[3 * piece + 2:3 * piece + 3, :]

    def x_copy(tile, act):
        def run(src, t):
            start = pl.multiple_of(t * tm, tm)
            act(pltpu.make_async_copy(src.at[pl.ds(start, tm), :], x_buf, x_sem))

        tile = tile + x_tile0
        pl.when(tile < xa_tiles)(lambda: run(xa_hbm, tile))
        pl.when(tile >= xa_tiles)(lambda: run(xb_hbm, tile - xa_tiles))

    def start(cp):
        cp.start()

    def wait(cp):
        cp.wait()

    def modulated(rows):
        sh = mod_ref[3 * piece:3 * piece + 1, :]
        sc = mod_ref[3 * piece + 1:3 * piece + 2, :]
        return (x_buf[rows, :] * (1.0 + sc) + sh).astype(BF16)

    def up(h):
        tf = wi_ref.shape[1] // 2
        parts = []
        for c in range(0, tf, MXU_WIDTH):
            width = min(MXU_WIDTH, tf - c)
            g = jnp.dot(h, wi_ref[:, c:c + width], preferred_element_type=F32)
            u = jnp.dot(h, wi_ref[:, tf + c:tf + c + width], preferred_element_type=F32)
            parts.append((jax.nn.silu(g) * u).astype(BF16))
        return jnp.concatenate(parts, axis=1)

    def gated_down(a, cols, wo=None):
        wo = wo_ref[:, cols].astype(BF16) if wo is None else wo
        return gate[:, cols] * jnp.dot(a, wo, preferred_element_type=F32)

    def layer_norm(y):
        return _layer_norm_rows(y, lng_ref[piece:piece + 1, :], lnb_ref[piece:piece + 1, :])

    col_slabs = _row_chunks(d, DOWN_SLAB)
    everything = slice(0, d)

    @pl.when(jnp.logical_and(i == 0, j == 0))
    def _():
        x_copy(i, start)

    @pl.when(j == 0)
    def _():
        x_copy(i, wait)

    if nj == 1:
        wo = wo_ref[...].astype(BF16)
        for rows in _row_chunks(tm, chunk):
            y = alpha * x_buf[rows, :] + gated_down(up(modulated(rows)), everything, wo)
            o_ref[rows, :] = layer_norm(y)
    else:
        @pl.when(j == 0)
        def _():
            h = modulated(slice(0, tm))
            h_ref[...] = h
            a = up(h)
            for cols in col_slabs:
                o_ref[:, cols] = alpha * x_buf[:, cols] + gated_down(a, cols)

        @pl.when(jnp.logical_and(j > 0, j < nj - 1))
        def _():
            a = up(h_ref[...])
            for cols in col_slabs:
                o_ref[:, cols] += gated_down(a, cols)

        @pl.when(j == nj - 1)
        def _():
            wo = wo_ref[...].astype(BF16)
            for rows in _row_chunks(tm, chunk):
                y = o_ref[rows, :] + gated_down(up(h_ref[rows, :]), everything, wo)
                o_ref[rows, :] = layer_norm(y)

    @pl.when(jnp.logical_and(j == 0, i + 1 < n_tiles))
    def _():
        x_copy(i + 1, start)


def _ffn_hidden_tile(f):
    return _pick_tile(f, 512)


def _pair_gate_up(wi):
    depth, d, f2 = wi.shape
    tf = _ffn_hidden_tile(f2 // 2)
    paired = wi.reshape(depth, d, 2, f2 // 2 // tf, tf).swapaxes(2, 3)
    return paired.reshape(depth, d, f2).astype(BF16)


def _ffn(xs, mod, ln_g, ln_b, wi, wo, *, layer, piece, alpha, row_of_tile, tm, first_tile=0, n_tiles=None):
    xa, xb = xs if isinstance(xs, tuple) else (xs, xs)
    d = xa.shape[1]
    f = wo.shape[1]
    tf = _ffn_hidden_tile(f)
    nj = f // tf
    l = layer
    xa_tiles = xa.shape[0] // tm
    if n_tiles is None:
        n_tiles = xa_tiles + (xb.shape[0] // tm if isinstance(xs, tuple) else 0) - first_tile
    if not isinstance(xs, tuple):
        assert first_tile + n_tiles <= xa_tiles
    return pl.pallas_call(
        functools.partial(_ffn_kernel, piece=piece, alpha=alpha, nj=nj, chunk=_pick_tile(tm, 256),
                          n_tiles=n_tiles, x_tile0=first_tile, xa_tiles=xa_tiles),
        grid=(n_tiles, nj),
        in_specs=[
            pl.BlockSpec(memory_space=pl.ANY),
            pl.BlockSpec(memory_space=pl.ANY),
            pl.BlockSpec((None, None, N_MOD, d), lambda i, j: (l, row_of_tile(i + first_tile, tm), 0, 0)),
            pl.BlockSpec((None, 3, d), lambda i, j: (l, 0, 0)),
            pl.BlockSpec((None, 3, d), lambda i, j: (l, 0, 0)),
            pl.BlockSpec((None, d, 2 * tf), lambda i, j: (l, 0, j)),
            pl.BlockSpec((None, tf, d), lambda i, j: (l, j, 0)),
        ],
        out_specs=pl.BlockSpec((tm, d), lambda i, j: (i, 0)),
        out_shape=jax.ShapeDtypeStruct((n_tiles * tm, d), F32),
        scratch_shapes=[pltpu.VMEM((tm, d), F32), pltpu.VMEM((tm, d), BF16), pltpu.SemaphoreType.DMA(())],
        compiler_params=_params("arbitrary", "arbitrary"),
        name=f"ffn{piece}",
    )(xa, xb, mod, ln_g, ln_b, wi, wo)


def _rope_slab(xs, cos, sin_signed):
    lane = lax.broadcasted_iota(jnp.int32, xs.shape, 1)
    partner = jnp.where(lane % 2 == 0, pltpu.roll(xs, LANES - 1, 1), pltpu.roll(xs, 1, 1))
    return xs * cos + partner * sin_signed


def _proj_kernel(x_ref, mod_ref, cos_ref, sin_ref, w_ref, q_ref, kv_ref, kf_ref, vf_ref, rest_ref):
    sh = mod_ref[3:4, :]
    sc = mod_ref[4:5, :]
    h = (x_ref[...] * (1.0 + sc) + sh).astype(BF16)
    cos = cos_ref[...]
    sin = sin_ref[...]

    q = jnp.dot(h, w_ref[:, :ATTN_WIDTH], preferred_element_type=F32)
    for s in range(ATTN_WIDTH // LANES):
        cols = slice(s * LANES, (s + 1) * LANES)
        q_ref[:, cols] = (_rope_slab(q[:, cols], cos, sin) * ATTN_SCALE).astype(BF16)

    kv = jnp.dot(h, w_ref[:, ATTN_WIDTH:ATTN_WIDTH + 2 * KV_WIDTH], preferred_element_type=F32)
    for s in range(KV_WIDTH // LANES):
        cols = slice(s * LANES, (s + 1) * LANES)
        k = _rope_slab(kv[:, cols], cos, sin)
        kf_ref[:, cols] = k
        kv_ref[:, cols] = k.astype(BF16)
    v = kv[:, KV_WIDTH:]
    vf_ref[...] = v
    kv_ref[:, KV_WIDTH:] = v.astype(BF16)

    rest_ref[...] = jnp.dot(h, w_ref[:, ATTN_WIDTH + 2 * KV_WIDTH:], preferred_element_type=F32)


def _proj(x, mod, cos_tab, sin_tab, w_in, *, layer, row_of_tile, tm, n_ctx_tokens, lat_seq):
    t, d = x.shape
    l = layer
    n_ctx_tiles = n_ctx_tokens // tm
    tiles_per_seq = lat_seq // tm

    def rope_idx(i):
        return (jnp.where(i < n_ctx_tiles, 0, 1 + jnp.maximum(i - n_ctx_tiles, 0) % tiles_per_seq), 0)

    def rows(width):
        return pl.BlockSpec((tm, width), lambda i: (i, 0))

    return pl.pallas_call(
        _proj_kernel,
        grid=(t // tm,),
        in_specs=[
            rows(d),
            pl.BlockSpec((None, None, N_MOD, d), lambda i: (l, row_of_tile(i, tm), 0, 0)),
            pl.BlockSpec((tm, LANES), rope_idx),
            pl.BlockSpec((tm, LANES), rope_idx),
            pl.BlockSpec((None, d, IN_WIDTH), lambda i: (l, 0, 0), pipeline_mode=pl.Buffered(1)),
        ],
        out_specs=[rows(ATTN_WIDTH), rows(2 * KV_WIDTH), rows(KV_WIDTH), rows(KV_WIDTH), rows(REST_WIDTH)],
        out_shape=[
            jax.ShapeDtypeStruct((t, ATTN_WIDTH), BF16),
            jax.ShapeDtypeStruct((t, 2 * KV_WIDTH), BF16),
            jax.ShapeDtypeStruct((t, KV_WIDTH), F32),
            jax.ShapeDtypeStruct((t, KV_WIDTH), F32),
            jax.ShapeDtypeStruct((t, REST_WIDTH), F32),
        ],
        compiler_params=_params("arbitrary"),
        name="in_proj",
    )(x, mod, cos_tab, sin_tab, w_in)


def _lane_replicate(x, g):
    kg = x[:, g * HEAD_DIM:(g + 1) * HEAD_DIM].astype(BF16)
    return jnp.concatenate([kg] * GROUP, axis=1)


def _attend(sink_ref, q_ref, o_ref, k_loc, v_loc, valid, past):
    lane = lax.broadcasted_iota(jnp.int32, (BLOCK, GROUP_WIDTH), 1)
    head_of_lane = lane // HEAD_DIM
    contract_last = (((1,), (1,)), ((), ()))
    for g in range(N_KV_HEADS):
        cols = slice(g * GROUP_WIDTH, (g + 1) * GROUP_WIDTH)
        qg = q_ref[:, cols]
        q_stack = jnp.concatenate(
            [jnp.where(head_of_lane == h, qg, jnp.zeros_like(qg)) for h in range(GROUP)], axis=0)
        s_loc = lax.dot_general(q_stack, _lane_replicate(k_loc, g), contract_last, preferred_element_type=F32)
        if past is not None:
            s_past = lax.dot_general(q_stack, past[0][g], contract_last, preferred_element_type=F32)
        e_loc, e_past, inv = [], [], []
        for h in range(GROUP):
            rows = slice(h * BLOCK, (h + 1) * BLOCK)
            sink = sink_ref[g * GROUP + h]
            sl = jnp.where(valid, s_loc[rows], NEG_INF)
            m = jnp.maximum(jnp.max(sl, axis=-1, keepdims=True), sink)
            if past is not None:
                sp = s_past[rows]
                m = jnp.maximum(m, jnp.max(sp, axis=-1, keepdims=True))
            el = jnp.exp(sl - m)
            denom = jnp.sum(el, axis=-1, keepdims=True) + jnp.exp(sink - m)
            e_loc.append(el.astype(BF16))
            if past is not None:
                ep = jnp.exp(sp - m)
                denom = denom + jnp.sum(ep, axis=-1, keepdims=True)
                e_past.append(ep.astype(BF16))
            inv.append(1.0 / denom)
        pv = jnp.dot(jnp.concatenate(e_loc, axis=0), _lane_replicate(v_loc, g), preferred_element_type=F32)
        if past is not None:
            pv = pv + jnp.dot(jnp.concatenate(e_past, axis=0), past[1][g], preferred_element_type=F32)
        out = jnp.zeros((BLOCK, GROUP_WIDTH), F32)
        for h in range(GROUP):
            rows = slice(h * BLOCK, (h + 1) * BLOCK)
            out = jnp.where(head_of_lane == h, pv[rows] * inv[h], out)
        o_ref[:, cols] = out.astype(o_ref.dtype)


def _attn_kernel(sink_ref, q_ref, kvp_ref, kvc_ref, kvn_ref, ck_ref, cv_ref, o_ref,
                 kpast_ref, vpast_ref, *, n_ctx_blocks, ctx_blocks_per_seq, lat_blocks_per_seq):
    n = pl.program_id(0)
    is_ctx = n < n_ctx_blocks
    pos = jnp.where(is_ctx, n % ctx_blocks_per_seq, (n - n_ctx_blocks) % lat_blocks_per_seq)
    last = jnp.where(is_ctx, ctx_blocks_per_seq - 1, lat_blocks_per_seq - 1)

    qi = lax.broadcasted_iota(jnp.int32, (BLOCK, 3 * BLOCK), 0)
    ki = lax.broadcasted_iota(jnp.int32, (BLOCK, 3 * BLOCK), 1) - BLOCK
    k_lo = jnp.where(pos > 0, -BLOCK, 0)
    k_hi = jnp.where(pos < last, 2 * BLOCK - 1, BLOCK - 1)
    in_seq = (ki >= k_lo) & (ki <= k_hi)

    kv_loc = jnp.concatenate([kvp_ref[...], kvc_ref[...], kvn_ref[...]], axis=0)
    k_loc = kv_loc[:, :KV_WIDTH]
    v_loc = kv_loc[:, KV_WIDTH:]

    @pl.when(is_ctx)
    def _():
        _attend(sink_ref, q_ref, o_ref, k_loc, v_loc, in_seq, None)

    @pl.when(jnp.logical_and(jnp.logical_not(is_ctx), pos == 0))
    def _():
        for g in range(N_KV_HEADS):
            kpast_ref[g] = _lane_replicate(ck_ref[...], g)
            vpast_ref[g] = _lane_replicate(cv_ref[...], g)

    @pl.when(jnp.logical_not(is_ctx))
    def _():
        band = in_seq & (jnp.abs(qi - ki) <= BLOCK)
        _attend(sink_ref, q_ref, o_ref, k_loc, v_loc, band, (kpast_ref, vpast_ref))


def _attention(q, kv, sink, cache_k, cache_v, *, layer, n_ctx_tokens, ctx_seq, lat_seq):
    t = q.shape[0]
    nblk = t // BLOCK
    n_ctx_blocks = n_ctx_tokens // BLOCK
    cps = ctx_seq // BLOCK
    lps = lat_seq // BLOCK
    past = cache_k.shape[2]
    l = layer

    def seq_pos(n):
        return jnp.where(n < n_ctx_blocks, n % cps, (n - n_ctx_blocks) % lps)

    def prev_blk(n):
        return (n - (seq_pos(n) > 0).astype(jnp.int32), 0)

    def next_blk(n):
        last = jnp.where(n < n_ctx_blocks, cps - 1, lps - 1)
        return (n + (seq_pos(n) < last).astype(jnp.int32), 0)

    def past_idx(n):
        return (jnp.maximum(n - n_ctx_blocks, 0) // lps, l, 0, 0)

    same = lambda n: (n, 0)
    kv_spec = lambda idx: pl.BlockSpec((BLOCK, 2 * KV_WIDTH), idx)
    return pl.pallas_call(
        functools.partial(_attn_kernel, n_ctx_blocks=n_ctx_blocks, ctx_blocks_per_seq=cps,
                          lat_blocks_per_seq=lps),
        grid=(nblk,),
        in_specs=[
            pl.BlockSpec(memory_space=pltpu.SMEM),
            pl.BlockSpec((BLOCK, ATTN_WIDTH), same),
            kv_spec(prev_blk), kv_spec(same), kv_spec(next_blk),
            pl.BlockSpec((None, None, past, KV_WIDTH), past_idx),
            pl.BlockSpec((None, None, past, KV_WIDTH), past_idx),
        ],
        out_specs=pl.BlockSpec((BLOCK, ATTN_WIDTH), same),
        out_shape=jax.ShapeDtypeStruct((t, ATTN_WIDTH), BF16),
        scratch_shapes=[pltpu.VMEM((N_KV_HEADS, past, GROUP_WIDTH), BF16),
                        pltpu.VMEM((N_KV_HEADS, past, GROUP_WIDTH), BF16)],
        compiler_params=_params("arbitrary"),
        name="attention",
    )(sink, q, kv, kv, kv, cache_k, cache_v)


def _mix_kernel(x_ref, mod_ref, lng_ref, lnb_ref, attn_ref, su_ref, sv_ref, cx_ref, cb_ref, cc_ref,
                cxp_ref, cxn_ref, ccp_ref, ccn_ref, gain_ref, ws_ref, sb_ref, cw_ref, wo_ref, o_ref,
                *, alpha, tm, ctx_seq, n_ctx_tiles, lat_tiles_per_seq):
    i = pl.program_id(0)
    is_ctx = i < n_ctx_tiles
    ctx_tiles_per_seq = max(ctx_seq // tm, 1)
    pos = jnp.where(is_ctx, i % ctx_tiles_per_seq, (i - n_ctx_tiles) % lat_tiles_per_seq)
    last = jnp.where(is_ctx, ctx_tiles_per_seq - 1, lat_tiles_per_seq - 1)

    hp = cc_ref[...] * cx_ref[...]
    hp_prev = jnp.where(pos > 0, ccp_ref[SUBLANES - 1:SUBLANES, :] * cxp_ref[SUBLANES - 1:SUBLANES, :], 0.0)
    hp_next = jnp.where(pos < last, ccn_ref[0:1, :] * cxn_ref[0:1, :], 0.0)
    row = lax.broadcasted_iota(jnp.int32, hp.shape, 0)
    if ctx_seq < tm:
        row = jnp.where(is_ctx, row & (ctx_seq - 1), row)
    last_row = jnp.where(is_ctx, min(ctx_seq, tm) - 1, tm - 1)
    up = jnp.where(row == 0, hp_prev, pltpu.roll(hp, 1, 0))
    dn = jnp.where(row == last_row, hp_next, pltpu.roll(hp, tm - 1, 0))
    y = cw_ref[0:1, :] * up + cw_ref[1:2, :] * hp + cw_ref[2:3, :] * dn
    conv = (cb_ref[...] * y).astype(BF16)

    gate = mod_ref[5:6, :]
    g = lng_ref[1:2, :]
    b = lnb_ref[1:2, :]
    def gating(rows):
        u = jax.nn.gelu(su_ref[rows, :])
        v = jax.nn.gelu(sv_ref[rows, :])
        mu = jnp.mean(v, axis=-1, keepdims=True)
        dv = v - mu
        var = jnp.mean(dv * dv, axis=-1, keepdims=True)
        vn = (dv * lax.rsqrt(var + LN_EPS) * gain_ref[...]).astype(BF16)
        parts = []
        for h in range(SGU_HEADS):
            cols = slice(h * SGU_HEAD_DIM, (h + 1) * SGU_HEAD_DIM)
            mixed = jnp.dot(ws_ref[h], vn[:, cols], preferred_element_type=F32) + sb_ref[:, cols]
            parts.append((u[:, cols] * mixed).astype(BF16))
        return jnp.concatenate(parts, axis=1)

    for rows in _row_chunks(tm, OUT_ROWS):
        sgu = jnp.concatenate([gating(slice(c, c + CHUNK)) for c in range(rows.start, rows.stop, CHUNK)], axis=0)
        mix = jnp.concatenate([attn_ref[rows, :], sgu, conv[rows, :]], axis=1)
        res = jnp.dot(mix, wo_ref[...], preferred_element_type=F32)
        o_ref[rows, :] = _layer_norm_rows(alpha * x_ref[rows, :] + gate * res, g, b)


def _mix(x, mod, ln_g, ln_b, attn, rest, gain, ws, sb_full, conv_w, w_out, *, layer, alpha, row_of_tile,
         n_ctx_tokens, ctx_seq, lat_seq):
    t, d = x.shape
    tm = _pick_tile(min(n_ctx_tokens, lat_seq), 512)
    l = layer
    halo_per_tile = tm // SUBLANES
    n_halo_blocks = t // SUBLANES
    su_c, sv_c, cx_c, cb_c, cc_c = range(5)
    assert lat_seq % tm == 0 and (ctx_seq % tm == 0 or (tm % ctx_seq == 0 and ctx_seq & (ctx_seq - 1) == 0))

    def col(c):
        return pl.BlockSpec((tm, 512), lambda i: (i, c))

    def halo_prev(c):
        return pl.BlockSpec((SUBLANES, 512), lambda i: (jnp.maximum(i * halo_per_tile - 1, 0), c))

    def halo_next(c):
        return pl.BlockSpec((SUBLANES, 512),
                            lambda i: (jnp.minimum((i + 1) * halo_per_tile, n_halo_blocks - 1), c))

    return pl.pallas_call(
        functools.partial(_mix_kernel, alpha=alpha, tm=tm, ctx_seq=ctx_seq, n_ctx_tiles=n_ctx_tokens // tm,
                          lat_tiles_per_seq=lat_seq // tm),
        grid=(t // tm,),
        in_specs=[
            pl.BlockSpec((tm, d), lambda i: (i, 0)),
            pl.BlockSpec((None, None, N_MOD, d), lambda i: (l, row_of_tile(i, tm), 0, 0)),
            pl.BlockSpec((None, 3, d), lambda i: (l, 0, 0)),
            pl.BlockSpec((None, 3, d), lambda i: (l, 0, 0)),
            pl.BlockSpec((tm, ATTN_WIDTH), lambda i: (i, 0)),
            col(su_c), col(sv_c), col(cx_c), col(cb_c), col(cc_c),
            halo_prev(cx_c), halo_next(cx_c), halo_prev(cc_c), halo_next(cc_c),
            pl.BlockSpec((None, 1, SGU_WIDTH), lambda i: (l, 0, 0)),
            pl.BlockSpec((None, SGU_HEADS, CHUNK, CHUNK), lambda i: (l, 0, 0, 0)),
            pl.BlockSpec((None, CHUNK, SGU_WIDTH), lambda i: (l, 0, 0)),
            pl.BlockSpec((None, 3, CONV_WIDTH), lambda i: (l, 0, 0)),
            pl.BlockSpec((None, MIX_WIDTH, d), lambda i: (l, 0, 0), pipeline_mode=pl.Buffered(1)),
        ],
        out_specs=pl.BlockSpec((tm, d), lambda i: (i, 0)),
        out_shape=jax.ShapeDtypeStruct((t, d), F32),
        compiler_params=_params("arbitrary"),
        name="mix_out",
    )(x, mod, ln_g, ln_b, attn, rest, rest, rest, rest, rest, rest, rest, rest, rest,
      gain, ws, sb_full, conv_w, w_out)


def _rope_tables(lat_seq, identity_rows):
    pos = jnp.arange(lat_seq)
    row = (pos // GRID_W).astype(F32)
    col = (pos % GRID_W).astype(F32)
    n_freq = HEAD_DIM // 4
    inv = ROPE_BASE ** (-jnp.arange(n_freq, dtype=F32) / n_freq)
    ang = jnp.concatenate([row[:, None] * inv, col[:, None] * inv], axis=-1)
    cos = jnp.cos(ang)
    sin = jnp.sin(ang)
    cos_pairs = jnp.repeat(cos, 2, axis=-1)
    sin_pairs = jnp.stack([-sin, sin], axis=-1).reshape(lat_seq, HEAD_DIM)
    reps = LANES // HEAD_DIM
    cos_tab = jnp.concatenate([jnp.ones((identity_rows, LANES), F32), jnp.tile(cos_pairs, (1, reps))], axis=0)
    sin_tab = jnp.concatenate([jnp.zeros((identity_rows, LANES), F32), jnp.tile(sin_pairs, (1, reps))], axis=0)
    return cos_tab, sin_tab


def kernel(x_prompt, x_sample, cache_k, cache_v, c, c_ctx, ada_w, ada_b, ffn1_wi, ffn1_wo, ffn2_wi, ffn2_wo,
           w_in, w_out, attn_sink, sgu_gain, sgu_ws, sgu_b, conv_w, ln_g, ln_b):
    batch, ctx_seq, d = x_prompt.shape
    lat_batch, lat_seq, _ = x_sample.shape
    depth = ada_w.shape[0]
    past = cache_k.shape[2]
    n_ctx_tokens = batch * ctx_seq
    assert lat_batch + 1 <= MOD_ROWS
    assert ctx_seq % BLOCK == 0 and lat_seq % BLOCK == 0 and lat_seq % GRID_W == 0

    alpha = (2 * depth) ** 0.25

    def row_of_tile(i, tm):
        first_lat = n_ctx_tokens // tm
        return jnp.where(i < first_lat, 0, 1 + jnp.maximum(i - first_lat, 0) // (lat_seq // tm))

    cv = jnp.zeros((MOD_ROWS, d), F32).at[0].set(c_ctx).at[1:1 + lat_batch].set(c)
    mod = _modulation(cv, ada_w, ada_b)

    wi1, wo1 = _pair_gate_up(ffn1_wi), ffn1_wo
    wi2, wo2 = _pair_gate_up(ffn2_wi), ffn2_wo
    w_in_b, w_out_b = w_in.astype(BF16), w_out.astype(BF16)
    ws_b = sgu_ws.astype(BF16)
    sb_full = jnp.repeat(jnp.swapaxes(sgu_b, 1, 2), SGU_HEAD_DIM, axis=2)
    gain = sgu_gain.reshape(depth, 1, SGU_WIDTH)
    ck = cache_k.reshape(lat_batch, depth, past, KV_WIDTH)
    cvv = cache_v.reshape(lat_batch, depth, past, KV_WIDTH)

    tm = _pick_tile(min(n_ctx_tokens, lat_seq), 512)
    tm_ffn = _pick_tile(min(n_ctx_tokens, lat_seq), 1024)
    n_ctx_tiles = n_ctx_tokens // tm_ffn
    n_lat_tiles = lat_batch * lat_seq // tm_ffn
    cos_tab, sin_tab = _rope_tables(lat_seq, tm)
    x = (x_prompt.reshape(n_ctx_tokens, d), x_sample.reshape(lat_batch * lat_seq, d))
    common = dict(row_of_tile=row_of_tile)
    ffn_common = dict(alpha=alpha, tm=tm_ffn, **common)
    ks_new, vs_new = [], []
    for l in range(depth):
        x = _ffn(x, mod, ln_g, ln_b, wi1, wo1, layer=l, piece=0, **ffn_common)
        q, kv, kf, vf, rest = _proj(x, mod, cos_tab, sin_tab, w_in_b, layer=l, tm=tm,
                                    n_ctx_tokens=n_ctx_tokens, lat_seq=lat_seq, **common)
        attn = _attention(q, kv, attn_sink[l], ck, cvv, layer=l, n_ctx_tokens=n_ctx_tokens,
                          ctx_seq=ctx_seq, lat_seq=lat_seq)
        x = _mix(x, mod, ln_g, ln_b, attn, rest, gain, ws_b, sb_full, conv_w, w_out_b, layer=l, alpha=alpha,
                 n_ctx_tokens=n_ctx_tokens, ctx_seq=ctx_seq, lat_seq=lat_seq, **common)
        if l < depth - 1:
            x = _ffn(x, mod, ln_g, ln_b, wi2, wo2, layer=l, piece=2, **ffn_common)
        ks_new.append(kf[:n_ctx_tokens].reshape(batch, ctx_seq, N_KV_HEADS, HEAD_DIM))
        vs_new.append(vf[:n_ctx_tokens].reshape(batch, ctx_seq, N_KV_HEADS, HEAD_DIM))

    last = dict(layer=depth - 1, piece=2, **ffn_common)
    y_prompt = _ffn(x, mod, ln_g, ln_b, wi2, wo2, n_tiles=n_ctx_tiles, **last)
    y_sample = _ffn(x, mod, ln_g, ln_b, wi2, wo2, first_tile=n_ctx_tiles, n_tiles=n_lat_tiles, **last)
    y_prompt = y_prompt.reshape(batch, ctx_seq, d)
    y_sample = y_sample.reshape(lat_batch, lat_seq, d)
    return (y_prompt, y_sample, jnp.stack(ks_new, axis=1), jnp.stack(vs_new, axis=1))
```

```python
import functools

import jax
import jax.numpy as jnp
from jax import lax
from jax.experimental import pallas as pl
from jax.experimental.pallas import tpu as pltpu

GRID_W = 64
BLOCK = 128
N_HEADS = 16
N_KV_HEADS = 4
GROUP = N_HEADS // N_KV_HEADS
HEAD_DIM = 64
ATTN_WIDTH = N_HEADS * HEAD_DIM
KV_WIDTH = N_KV_HEADS * HEAD_DIM
GROUP_WIDTH = GROUP * HEAD_DIM
ATTN_SCALE = HEAD_DIM ** -0.5
ROPE_BASE = 10000.0
SGU_HEADS = 4
SGU_HEAD_DIM = 128
SGU_WIDTH = SGU_HEADS * SGU_HEAD_DIM
CHUNK = 128
CONV_WIDTH = 512
MIX_WIDTH = ATTN_WIDTH + SGU_WIDTH + CONV_WIDTH
IN_WIDTH = ATTN_WIDTH + 2 * KV_WIDTH + 2 * SGU_WIDTH + 3 * CONV_WIDTH
N_MOD = 9
LN_EPS = 1e-5
NEG_INF = -1e30

LANES = 128
SUBLANES = 8
MXU_WIDTH = 256
DOWN_SLAB = 512
OUT_ROWS = 256
MOD_ROWS = 16
VMEM_LIMIT_BYTES = 56 * 1024 * 1024

BF16 = jnp.bfloat16
F32 = jnp.float32


def _params(*sem):
    return pltpu.CompilerParams(dimension_semantics=sem, vmem_limit_bytes=VMEM_LIMIT_BYTES)


def _pick_tile(n, pref):
    t = min(n, pref)
    while n % t:
        t //= 2
    return t


def _row_chunks(n, size):
    return [slice(r, min(r + size, n)) for r in range(0, n, size)]


def _layer_norm_rows(y, g, b):
    mu = jnp.mean(y, axis=-1, keepdims=True)
    d = y - mu
    var = jnp.mean(d * d, axis=-1, keepdims=True)
    return d * lax.rsqrt(var + LN_EPS) * g + b


def _mod_kernel(cv_ref, w_ref, b_ref, o_ref):
    a = jax.nn.silu(cv_ref[...]).astype(BF16)
    o_ref[...] = jnp.dot(a, w_ref[...].astype(BF16), preferred_element_type=F32) + b_ref[...]


def _modulation(cv, ada_w, ada_b):
    depth, d, nd = ada_w.shape
    tn = _pick_tile(nd, 1024)
    out = pl.pallas_call(
        _mod_kernel,
        grid=(depth, nd // tn),
        in_specs=[
            pl.BlockSpec((MOD_ROWS, d), lambda l, n: (0, 0)),
            pl.BlockSpec((None, d, tn), lambda l, n: (l, 0, n)),
            pl.BlockSpec((None, 1, tn), lambda l, n: (l, 0, n)),
        ],
        out_specs=pl.BlockSpec((None, MOD_ROWS, tn), lambda l, n: (l, 0, n)),
        out_shape=jax.ShapeDtypeStruct((depth, MOD_ROWS, nd), F32),
        compiler_params=_params("arbitrary", "arbitrary"),
        name="modulation",
    )(cv, ada_w, ada_b.reshape(depth, 1, nd))
    return out.reshape(depth, MOD_ROWS, N_MOD, d)


def _ffn_kernel(xa_hbm, xb_hbm, mod_ref, lng_ref, lnb_ref, wig_ref, wiu_ref, wo_ref, o_ref, x_buf, h_ref, x_sem,
                *, piece, alpha, nj, chunk, n_tiles, x_tile0, xa_tiles):
    i = pl.program_id(0)
    j = pl.program_id(1)
    tm, d = o_ref.shape
    gate = 0.5 * mod_ref[3 * piece + 2:3 * piece + 3, :]

    def x_copy(tile, act):
        def run(src, t):
            start = pl.multiple_of(t * tm, tm)
            act(pltpu.make_async_copy(src.at[pl.ds(start, tm), :], x_buf, x_sem))

        tile = tile + x_tile0
        pl.when(tile < xa_tiles)(lambda: run(xa_hbm, tile))
        pl.when(tile >= xa_tiles)(lambda: run(xb_hbm, tile - xa_tiles))

    def start(cp):
        cp.start()

    def wait(cp):
        cp.wait()

    def modulated(rows):
        sh = mod_ref[3 * piece:3 * piece + 1, :]
        sc = mod_ref[3 * piece + 1:3 * piece + 2, :]
        return (x_buf[rows, :] * (1.0 + sc) + sh).astype(BF16)

    def up(h):
        tf = wig_ref.shape[1]
        parts = []
        for c in range(0, tf, MXU_WIDTH):
            cols = slice(c, min(c + MXU_WIDTH, tf))
            g = jnp.dot(h, wig_ref[:, cols], preferred_element_type=F32)
            u = jnp.dot(h, wiu_ref[:, cols], preferred_element_type=F32)
            parts.append((jax.nn.silu(g) * u).astype(BF16))
        return jnp.concatenate(parts, axis=1)

    def gated_down(a, cols):
        return gate[:, cols] * jnp.dot(a, wo_ref[:, cols], preferred_element_type=F32)

    def layer_norm(y):
        return _layer_norm_rows(y, lng_ref[piece:piece + 1, :], lnb_ref[piece:piece + 1, :])

    col_slabs = _row_chunks(d, DOWN_SLAB)
    everything = slice(0, d)

    @pl.when(jnp.logical_and(i == 0, j == 0))
    def _():
        x_copy(i, start)

    @pl.when(j == 0)
    def _():
        x_copy(i, wait)

    if nj == 1:
        for rows in _row_chunks(tm, chunk):
            y = alpha * x_buf[rows, :] + gated_down(up(modulated(rows)), everything)
            o_ref[rows, :] = layer_norm(y)
    else:
        @pl.when(j == 0)
        def _():
            h = modulated(slice(0, tm))
            h_ref[...] = h
            a = up(h)
            for cols in col_slabs:
                o_ref[:, cols] = alpha * x_buf[:, cols] + gated_down(a, cols)

        @pl.when(jnp.logical_and(j > 0, j < nj - 1))
        def _():
            a = up(h_ref[...])
            for cols in col_slabs:
                o_ref[:, cols] += gated_down(a, cols)

        @pl.when(j == nj - 1)
        def _():
            for rows in _row_chunks(tm, chunk):
                y = o_ref[rows, :] + gated_down(up(h_ref[rows, :]), everything)
                o_ref[rows, :] = layer_norm(y)

    @pl.when(jnp.logical_and(j == 0, i + 1 < n_tiles))
    def _():
        x_copy(i + 1, start)


def _ffn(xs, mod, ln_g, ln_b, wi, wo, *, layer, piece, alpha, row_of_tile, tm, first_tile=0, n_tiles=None):
    xa, xb = xs if isinstance(xs, tuple) else (xs, xs)
    d = xa.shape[1]
    f = wo.shape[1]
    tf = _pick_tile(f, 512)
    nj = f // tf
    l = layer
    xa_tiles = xa.shape[0] // tm
    if n_tiles is None:
        n_tiles = xa_tiles + (xb.shape[0] // tm if isinstance(xs, tuple) else 0) - first_tile
    if not isinstance(xs, tuple):
        assert first_tile + n_tiles <= xa_tiles
    return pl.pallas_call(
        functools.partial(_ffn_kernel, piece=piece, alpha=alpha, nj=nj, chunk=_pick_tile(tm, 256),
                          n_tiles=n_tiles, x_tile0=first_tile, xa_tiles=xa_tiles),
        grid=(n_tiles, nj),
        in_specs=[
            pl.BlockSpec(memory_space=pl.ANY),
            pl.BlockSpec(memory_space=pl.ANY),
            pl.BlockSpec((None, None, N_MOD, d), lambda i, j: (l, row_of_tile(i + first_tile, tm), 0, 0)),
            pl.BlockSpec((None, 3, d), lambda i, j: (l, 0, 0)),
            pl.BlockSpec((None, 3, d), lambda i, j: (l, 0, 0)),
            pl.BlockSpec((None, d, tf), lambda i, j: (l, 0, j)),
            pl.BlockSpec((None, d, tf), lambda i, j: (l, 0, j + nj)),
            pl.BlockSpec((None, tf, d), lambda i, j: (l, j, 0)),
        ],
        out_specs=pl.BlockSpec((tm, d), lambda i, j: (i, 0)),
        out_shape=jax.ShapeDtypeStruct((n_tiles * tm, d), F32),
        scratch_shapes=[pltpu.VMEM((tm, d), F32), pltpu.VMEM((tm, d), BF16), pltpu.SemaphoreType.DMA(())],
        compiler_params=_params("arbitrary", "arbitrary"),
        name=f"ffn{piece}",
    )(xa, xb, mod, ln_g, ln_b, wi, wi, wo)


def _rope_slab(xs, cos, sin_signed):
    lane = lax.broadcasted_iota(jnp.int32, xs.shape, 1)
    partner = jnp.where(lane % 2 == 0, pltpu.roll(xs, LANES - 1, 1), pltpu.roll(xs, 1, 1))
    return xs * cos + partner * sin_signed


def _proj_kernel(x_ref, mod_ref, cos_ref, sin_ref, w_ref, gain_ref, ws_ref, sb_ref,
                 q_ref, kv_ref, kf_ref, vf_ref, sgu_ref, hp_ref, cb_ref):
    sh = mod_ref[3:4, :]
    sc = mod_ref[4:5, :]
    h = (x_ref[...] * (1.0 + sc) + sh).astype(BF16)
    cos = cos_ref[...]
    sin = sin_ref[...]

    q = jnp.dot(h, w_ref[:, :ATTN_WIDTH], preferred_element_type=F32)
    for s in range(ATTN_WIDTH // LANES):
        cols = slice(s * LANES, (s + 1) * LANES)
        q_ref[:, cols] = (_rope_slab(q[:, cols], cos, sin) * ATTN_SCALE).astype(BF16)

    kv = jnp.dot(h, w_ref[:, ATTN_WIDTH:ATTN_WIDTH + 2 * KV_WIDTH], preferred_element_type=F32)
    for s in range(KV_WIDTH // LANES):
        cols = slice(s * LANES, (s + 1) * LANES)
        k = _rope_slab(kv[:, cols], cos, sin)
        kf_ref[:, cols] = k
        kv_ref[:, cols] = k.astype(BF16)
    v = kv[:, KV_WIDTH:]
    vf_ref[...] = v
    kv_ref[:, KV_WIDTH:] = v.astype(BF16)

    base = ATTN_WIDTH + 2 * KV_WIDTH
    su = jnp.dot(h, w_ref[:, base:base + SGU_WIDTH], preferred_element_type=F32)
    sv = jnp.dot(h, w_ref[:, base + SGU_WIDTH:base + 2 * SGU_WIDTH], preferred_element_type=F32)
    conv_in = jnp.dot(h, w_ref[:, base + 2 * SGU_WIDTH:], preferred_element_type=F32)
    cb_ref[...] = conv_in[:, CONV_WIDTH:2 * CONV_WIDTH]
    hp_ref[...] = conv_in[:, 2 * CONV_WIDTH:] * conv_in[:, :CONV_WIDTH]

    u = jax.nn.gelu(su)
    v = jax.nn.gelu(sv)
    mu = jnp.mean(v, axis=-1, keepdims=True)
    dv = v - mu
    var = jnp.mean(dv * dv, axis=-1, keepdims=True)
    vn = (dv * lax.rsqrt(var + LN_EPS) * gain_ref[...]).astype(BF16)
    for rows in _row_chunks(x_ref.shape[0], CHUNK):
        for head in range(SGU_HEADS):
            cols = slice(head * SGU_HEAD_DIM, (head + 1) * SGU_HEAD_DIM)
            mixed = jnp.dot(ws_ref[head], vn[rows, cols], preferred_element_type=F32) + sb_ref[:, cols]
            sgu_ref[rows, cols] = (u[rows, cols] * mixed).astype(BF16)


def _proj(x, mod, cos_tab, sin_tab, w_in, gain, ws, sb_full, *, layer, row_of_tile, tm, n_ctx_tokens, lat_seq):
    t, d = x.shape
    l = layer
    n_ctx_tiles = n_ctx_tokens // tm
    tiles_per_seq = lat_seq // tm

    def rope_idx(i):
        return (jnp.where(i < n_ctx_tiles, 0, 1 + jnp.maximum(i - n_ctx_tiles, 0) % tiles_per_seq), 0)

    def rows(width):
        return pl.BlockSpec((tm, width), lambda i: (i, 0))

    return pl.pallas_call(
        _proj_kernel,
        grid=(t // tm,),
        in_specs=[
            rows(d),
            pl.BlockSpec((None, None, N_MOD, d), lambda i: (l, row_of_tile(i, tm), 0, 0)),
            pl.BlockSpec((tm, LANES), rope_idx),
            pl.BlockSpec((tm, LANES), rope_idx),
            pl.BlockSpec((None, d, IN_WIDTH), lambda i: (l, 0, 0), pipeline_mode=pl.Buffered(1)),
            pl.BlockSpec((None, 1, SGU_WIDTH), lambda i: (l, 0, 0)),
            pl.BlockSpec((None, SGU_HEADS, CHUNK, CHUNK), lambda i: (l, 0, 0, 0)),
            pl.BlockSpec((None, CHUNK, SGU_WIDTH), lambda i: (l, 0, 0)),
        ],
        out_specs=[rows(ATTN_WIDTH), rows(2 * KV_WIDTH), rows(KV_WIDTH), rows(KV_WIDTH), rows(SGU_WIDTH),
                   rows(CONV_WIDTH), rows(CONV_WIDTH)],
        out_shape=[
            jax.ShapeDtypeStruct((t, ATTN_WIDTH), BF16),
            jax.ShapeDtypeStruct((t, 2 * KV_WIDTH), BF16),
            jax.ShapeDtypeStruct((t, KV_WIDTH), F32),
            jax.ShapeDtypeStruct((t, KV_WIDTH), F32),
            jax.ShapeDtypeStruct((t, SGU_WIDTH), BF16),
            jax.ShapeDtypeStruct((t, CONV_WIDTH), F32),
            jax.ShapeDtypeStruct((t, CONV_WIDTH), F32),
        ],
        compiler_params=_params("arbitrary"),
        name="in_proj",
    )(x, mod, cos_tab, sin_tab, w_in, gain, ws, sb_full)


def _lane_replicate(x, g):
    kg = x[:, g * HEAD_DIM:(g + 1) * HEAD_DIM].astype(BF16)
    return jnp.concatenate([kg] * GROUP, axis=1)


def _attend(sink_ref, q_ref, o_ref, k_loc, v_loc, valid, past):
    lane = lax.broadcasted_iota(jnp.int32, (BLOCK, GROUP_WIDTH), 1)
    head_of_lane = lane // HEAD_DIM
    contract_last = (((1,), (1,)), ((), ()))
    for g in range(N_KV_HEADS):
        cols = slice(g * GROUP_WIDTH, (g + 1) * GROUP_WIDTH)
        qg = q_ref[:, cols]
        q_stack = jnp.concatenate(
            [jnp.where(head_of_lane == h, qg, jnp.zeros_like(qg)) for h in range(GROUP)], axis=0)
        s_loc = lax.dot_general(q_stack, _lane_replicate(k_loc, g), contract_last, preferred_element_type=F32)
        if past is not None:
            s_past = lax.dot_general(q_stack, past[0][g], contract_last, preferred_element_type=F32)
        e_loc, e_past, inv = [], [], []
        for h in range(GROUP):
            rows = slice(h * BLOCK, (h + 1) * BLOCK)
            sink = sink_ref[g * GROUP + h]
            sl = jnp.where(valid, s_loc[rows], NEG_INF)
            m = jnp.maximum(jnp.max(sl, axis=-1, keepdims=True), sink)
            if past is not None:
                sp = s_past[rows]
                m = jnp.maximum(m, jnp.max(sp, axis=-1, keepdims=True))
            el = jnp.exp(sl - m)
            denom = jnp.sum(el, axis=-1, keepdims=True) + jnp.exp(sink - m)
            e_loc.append(el.astype(BF16))
            if past is not None:
                ep = jnp.exp(sp - m)
                denom = denom + jnp.sum(ep, axis=-1, keepdims=True)
                e_past.append(ep.astype(BF16))
            inv.append(1.0 / denom)
        pv = jnp.dot(jnp.concatenate(e_loc, axis=0), _lane_replicate(v_loc, g), preferred_element_type=F32)
        if past is not None:
            pv = pv + jnp.dot(jnp.concatenate(e_past, axis=0), past[1][g], preferred_element_type=F32)
        out = jnp.zeros((BLOCK, GROUP_WIDTH), F32)
        for h in range(GROUP):
            rows = slice(h * BLOCK, (h + 1) * BLOCK)
            out = jnp.where(head_of_lane == h, pv[rows] * inv[h], out)
        o_ref[:, cols] = out.astype(o_ref.dtype)


def _attn_kernel(sink_ref, q_ref, kvp_ref, kvc_ref, kvn_ref, ck_ref, cv_ref, o_ref,
                 kpast_ref, vpast_ref, *, n_ctx_blocks, ctx_blocks_per_seq, lat_blocks_per_seq):
    n = pl.program_id(0)
    is_ctx = n < n_ctx_blocks
    pos = jnp.where(is_ctx, n % ctx_blocks_per_seq, (n - n_ctx_blocks) % lat_blocks_per_seq)
    last = jnp.where(is_ctx, ctx_blocks_per_seq - 1, lat_blocks_per_seq - 1)

    qi = lax.broadcasted_iota(jnp.int32, (BLOCK, 3 * BLOCK), 0)
    ki = lax.broadcasted_iota(jnp.int32, (BLOCK, 3 * BLOCK), 1) - BLOCK
    k_lo = jnp.where(pos > 0, -BLOCK, 0)
    k_hi = jnp.where(pos < last, 2 * BLOCK - 1, BLOCK - 1)
    in_seq = (ki >= k_lo) & (ki <= k_hi)

    kv_loc = jnp.concatenate([kvp_ref[...], kvc_ref[...], kvn_ref[...]], axis=0)
    k_loc = kv_loc[:, :KV_WIDTH]
    v_loc = kv_loc[:, KV_WIDTH:]

    @pl.when(is_ctx)
    def _():
        _attend(sink_ref, q_ref, o_ref, k_loc, v_loc, in_seq, None)

    @pl.when(jnp.logical_and(jnp.logical_not(is_ctx), pos == 0))
    def _():
        for g in range(N_KV_HEADS):
            kpast_ref[g] = _lane_replicate(ck_ref[...], g)
            vpast_ref[g] = _lane_replicate(cv_ref[...], g)

    @pl.when(jnp.logical_not(is_ctx))
    def _():
        band = in_seq & (jnp.abs(qi - ki) <= BLOCK)
        _attend(sink_ref, q_ref, o_ref, k_loc, v_loc, band, (kpast_ref, vpast_ref))


def _attention(q, kv, sink, cache_k, cache_v, *, layer, n_ctx_tokens, ctx_seq, lat_seq):
    t = q.shape[0]
    nblk = t // BLOCK
    n_ctx_blocks = n_ctx_tokens // BLOCK
    cps = ctx_seq // BLOCK
    lps = lat_seq // BLOCK
    past = cache_k.shape[2]
    l = layer

    def seq_pos(n):
        return jnp.where(n < n_ctx_blocks, n % cps, (n - n_ctx_blocks) % lps)

    def prev_blk(n):
        return (n - (seq_pos(n) > 0).astype(jnp.int32), 0)

    def next_blk(n):
        last = jnp.where(n < n_ctx_blocks, cps - 1, lps - 1)
        return (n + (seq_pos(n) < last).astype(jnp.int32), 0)

    def past_idx(n):
        return (jnp.maximum(n - n_ctx_blocks, 0) // lps, l, 0, 0)

    same = lambda n: (n, 0)
    kv_spec = lambda idx: pl.BlockSpec((BLOCK, 2 * KV_WIDTH), idx)
    return pl.pallas_call(
        functools.partial(_attn_kernel, n_ctx_blocks=n_ctx_blocks, ctx_blocks_per_seq=cps,
                          lat_blocks_per_seq=lps),
        grid=(nblk,),
        in_specs=[
            pl.BlockSpec(memory_space=pltpu.SMEM),
            pl.BlockSpec((BLOCK, ATTN_WIDTH), same),
            kv_spec(prev_blk), kv_spec(same), kv_spec(next_blk),
            pl.BlockSpec((None, None, past, KV_WIDTH), past_idx),
            pl.BlockSpec((None, None, past, KV_WIDTH), past_idx),
        ],
        out_specs=pl.BlockSpec((BLOCK, ATTN_WIDTH), same),
        out_shape=jax.ShapeDtypeStruct((t, ATTN_WIDTH), BF16),
        scratch_shapes=[pltpu.VMEM((N_KV_HEADS, past, GROUP_WIDTH), BF16),
                        pltpu.VMEM((N_KV_HEADS, past, GROUP_WIDTH), BF16)],
        compiler_params=_params("arbitrary"),
        name="attention",
    )(sink, q, kv, kv, kv, cache_k, cache_v)


def _mix_kernel(x_ref, mod_ref, lng_ref, lnb_ref, attn_ref, sgu_ref, hp_ref, cb_ref, hpp_ref, hpn_ref,
                cw_ref, wo_ref, o_ref,
                *, alpha, tm, ctx_seq, n_ctx_tiles, lat_tiles_per_seq):
    i = pl.program_id(0)
    is_ctx = i < n_ctx_tiles
    ctx_tiles_per_seq = max(ctx_seq // tm, 1)
    pos = jnp.where(is_ctx, i % ctx_tiles_per_seq, (i - n_ctx_tiles) % lat_tiles_per_seq)
    last = jnp.where(is_ctx, ctx_tiles_per_seq - 1, lat_tiles_per_seq - 1)

    hp = hp_ref[...]
    hp_prev = jnp.where(pos > 0, hpp_ref[SUBLANES - 1:SUBLANES, :], 0.0)
    hp_next = jnp.where(pos < last, hpn_ref[0:1, :], 0.0)
    row = lax.broadcasted_iota(jnp.int32, hp.shape, 0)
    if ctx_seq < tm:
        row = jnp.where(is_ctx, row & (ctx_seq - 1), row)
    last_row = jnp.where(is_ctx, min(ctx_seq, tm) - 1, tm - 1)
    up = jnp.where(row == 0, hp_prev, pltpu.roll(hp, 1, 0))
    dn = jnp.where(row == last_row, hp_next, pltpu.roll(hp, tm - 1, 0))
    y = cw_ref[0:1, :] * up + cw_ref[1:2, :] * hp + cw_ref[2:3, :] * dn
    conv = (cb_ref[...] * y).astype(BF16)

    gate = mod_ref[5:6, :]
    g = lng_ref[1:2, :]
    b = lnb_ref[1:2, :]
    for rows in _row_chunks(tm, OUT_ROWS):
        mix = jnp.concatenate([attn_ref[rows, :], sgu_ref[rows, :], conv[rows, :]], axis=1)
        res = jnp.dot(mix, wo_ref[...], preferred_element_type=F32)
        o_ref[rows, :] = _layer_norm_rows(alpha * x_ref[rows, :] + gate * res, g, b)


def _mix(x, mod, ln_g, ln_b, attn, sgu, hp, cb, conv_w, w_out, *, layer, alpha, row_of_tile,
         n_ctx_tokens, ctx_seq, lat_seq):
    t, d = x.shape
    tm = _pick_tile(min(n_ctx_tokens, lat_seq), 512)
    l = layer
    halo_per_tile = tm // SUBLANES
    n_halo_blocks = t // SUBLANES
    assert lat_seq % tm == 0 and (ctx_seq % tm == 0 or (tm % ctx_seq == 0 and ctx_seq & (ctx_seq - 1) == 0))

    def rows(width):
        return pl.BlockSpec((tm, width), lambda i: (i, 0))

    halo_prev = pl.BlockSpec((SUBLANES, CONV_WIDTH), lambda i: (jnp.maximum(i * halo_per_tile - 1, 0), 0))
    halo_next = pl.BlockSpec((SUBLANES, CONV_WIDTH),
                             lambda i: (jnp.minimum((i + 1) * halo_per_tile, n_halo_blocks - 1), 0))

    return pl.pallas_call(
        functools.partial(_mix_kernel, alpha=alpha, tm=tm, ctx_seq=ctx_seq, n_ctx_tiles=n_ctx_tokens // tm,
                          lat_tiles_per_seq=lat_seq // tm),
        grid=(t // tm,),
        in_specs=[
            pl.BlockSpec((tm, d), lambda i: (i, 0)),
            pl.BlockSpec((None, None, N_MOD, d), lambda i: (l, row_of_tile(i, tm), 0, 0)),
            pl.BlockSpec((None, 3, d), lambda i: (l, 0, 0)),
            pl.BlockSpec((None, 3, d), lambda i: (l, 0, 0)),
            rows(ATTN_WIDTH), rows(SGU_WIDTH), rows(CONV_WIDTH), rows(CONV_WIDTH), halo_prev, halo_next,
            pl.BlockSpec((None, 3, CONV_WIDTH), lambda i: (l, 0, 0)),
            pl.BlockSpec((None, MIX_WIDTH, d), lambda i: (l, 0, 0), pipeline_mode=pl.Buffered(1)),
        ],
        out_specs=pl.BlockSpec((tm, d), lambda i: (i, 0)),
        out_shape=jax.ShapeDtypeStruct((t, d), F32),
        compiler_params=_params("arbitrary"),
        name="mix_out",
    )(x, mod, ln_g, ln_b, attn, sgu, hp, cb, hp, hp, conv_w, w_out)


def _rope_tables(lat_seq, identity_rows):
    pos = jnp.arange(lat_seq)
    row = (pos // GRID_W).astype(F32)
    col = (pos % GRID_W).astype(F32)
    n_freq = HEAD_DIM // 4
    inv = ROPE_BASE ** (-jnp.arange(n_freq, dtype=F32) / n_freq)
    ang = jnp.concatenate([row[:, None] * inv, col[:, None] * inv], axis=-1)
    cos = jnp.cos(ang)
    sin = jnp.sin(ang)
    cos_pairs = jnp.repeat(cos, 2, axis=-1)
    sin_pairs = jnp.stack([-sin, sin], axis=-1).reshape(lat_seq, HEAD_DIM)
    reps = LANES // HEAD_DIM
    cos_tab = jnp.concatenate([jnp.ones((identity_rows, LANES), F32), jnp.tile(cos_pairs, (1, reps))], axis=0)
    sin_tab = jnp.concatenate([jnp.zeros((identity_rows, LANES), F32), jnp.tile(sin_pairs, (1, reps))], axis=0)
    return cos_tab, sin_tab


def kernel(x_prompt, x_sample, cache_k, cache_v, c, c_ctx, ada_w, ada_b, ffn1_wi, ffn1_wo, ffn2_wi, ffn2_wo,
           w_in, w_out, attn_sink, sgu_gain, sgu_ws, sgu_b, conv_w, ln_g, ln_b):
    batch, ctx_seq, d = x_prompt.shape
    lat_batch, lat_seq, _ = x_sample.shape
    depth = ada_w.shape[0]
    past = cache_k.shape[2]
    n_ctx_tokens = batch * ctx_seq
    assert lat_batch + 1 <= MOD_ROWS
    assert ctx_seq % BLOCK == 0 and lat_seq % BLOCK == 0 and lat_seq % GRID_W == 0

    alpha = (2 * depth) ** 0.25

    def row_of_tile(i, tm):
        first_lat = n_ctx_tokens // tm
        return jnp.where(i < first_lat, 0, 1 + jnp.maximum(i - first_lat, 0) // (lat_seq // tm))

    cv = jnp.zeros((MOD_ROWS, d), F32).at[0].set(c_ctx).at[1:1 + lat_batch].set(c)
    mod = _modulation(cv, ada_w, ada_b)

    wi1, wo1 = ffn1_wi.astype(BF16), ffn1_wo.astype(BF16)
    wi2, wo2 = ffn2_wi.astype(BF16), ffn2_wo.astype(BF16)
    w_in_b, w_out_b = w_in.astype(BF16), w_out.astype(BF16)
    ws_b = sgu_ws.astype(BF16)
    sb_full = jnp.repeat(jnp.swapaxes(sgu_b, 1, 2), SGU_HEAD_DIM, axis=2)
    gain = sgu_gain.reshape(depth, 1, SGU_WIDTH)
    ck = cache_k.reshape(lat_batch, depth, past, KV_WIDTH)
    cvv = cache_v.reshape(lat_batch, depth, past, KV_WIDTH)

    tm = _pick_tile(min(n_ctx_tokens, lat_seq), 512)
    tm_ffn = _pick_tile(min(n_ctx_tokens, lat_seq), 1024)
    n_ctx_tiles = n_ctx_tokens // tm_ffn
    n_lat_tiles = lat_batch * lat_seq // tm_ffn
    cos_tab, sin_tab = _rope_tables(lat_seq, tm)
    x = (x_prompt.reshape(n_ctx_tokens, d), x_sample.reshape(lat_batch * lat_seq, d))
    common = dict(row_of_tile=row_of_tile)
    ffn_common = dict(alpha=alpha, tm=tm_ffn, **common)
    ks_new, vs_new = [], []
    for l in range(depth):
        x = _ffn(x, mod, ln_g, ln_b, wi1, wo1, layer=l, piece=0, **ffn_common)
        q, kv, kf, vf, sgu, hp, cb = _proj(x, mod, cos_tab, sin_tab, w_in_b, gain, ws_b, sb_full, layer=l, tm=tm,
                                           n_ctx_tokens=n_ctx_tokens, lat_seq=lat_seq, **common)
        attn = _attention(q, kv, attn_sink[l], ck, cvv, layer=l, n_ctx_tokens=n_ctx_tokens,
                          ctx_seq=ctx_seq, lat_seq=lat_seq)
        x = _mix(x, mod, ln_g, ln_b, attn, sgu, hp, cb, conv_w, w_out_b, layer=l, alpha=alpha,
                 n_ctx_tokens=n_ctx_tokens, ctx_seq=ctx_seq, lat_seq=lat_seq, **common)
        if l < depth - 1:
            x = _ffn(x, mod, ln_g, ln_b, wi2, wo2, layer=l, piece=2, **ffn_common)
        ks_new.append(kf[:n_ctx_tokens].reshape(batch, ctx_seq, N_KV_HEADS, HEAD_DIM))
        vs_new.append(vf[:n_ctx_tokens].reshape(batch, ctx_seq, N_KV_HEADS, HEAD_DIM))

    last = dict(layer=depth - 1, piece=2, **ffn_common)
    y_prompt = _ffn(x, mod, ln_g, ln_b, wi2, wo2, n_tiles=n_ctx_tiles, **last)
    y_sample = _ffn(x, mod, ln_g, ln_b, wi2, wo2, first_tile=n_ctx_tiles, n_tiles=n_lat_tiles, **last)
    y_prompt = y_prompt.reshape(batch, ctx_seq, d)
    y_sample = y_sample.reshape(lat_batch, lat_seq, d)
    return (y_prompt, y_sample, jnp.stack(ks_new, axis=1), jnp.stack(vs_new, axis=1))
```

```python
import functools

import jax
import jax.numpy as jnp
from jax import lax
from jax.experimental import pallas as pl
from jax.experimental.pallas import tpu as pltpu

GRID_W = 64
BLOCK = 128
N_HEADS = 16
N_KV_HEADS = 4
GROUP = N_HEADS // N_KV_HEADS
HEAD_DIM = 64
ATTN_WIDTH = N_HEADS * HEAD_DIM
KV_WIDTH = N_KV_HEADS * HEAD_DIM
GROUP_WIDTH = GROUP * HEAD_DIM
ATTN_SCALE = HEAD_DIM ** -0.5
ROPE_BASE = 10000.0
SGU_HEADS = 4
SGU_HEAD_DIM = 128
SGU_WIDTH = SGU_HEADS * SGU_HEAD_DIM
CHUNK = 128
CONV_WIDTH = 512
MIX_WIDTH = ATTN_WIDTH + SGU_WIDTH + CONV_WIDTH
IN_WIDTH = ATTN_WIDTH + 2 * KV_WIDTH + 2 * SGU_WIDTH + 3 * CONV_WIDTH
N_MOD = 9
LN_EPS = 1e-5
NEG_INF = -1e30

LANES = 128
SUBLANES = 8
MXU_WIDTH = 256
DOWN_SLAB = 512
OUT_ROWS = 256
MOD_ROWS = 16
VMEM_LIMIT_BYTES = 56 * 1024 * 1024

BF16 = jnp.bfloat16
F32 = jnp.float32


def _params(*sem):
    return pltpu.CompilerParams(dimension_semantics=sem, vmem_limit_bytes=VMEM_LIMIT_BYTES)


def _pick_tile(n, pref):
    t = min(n, pref)
    while n % t:
        t //= 2
    return t


def _row_chunks(n, size):
    return [slice(r, min(r + size, n)) for r in range(0, n, size)]


def _layer_norm_rows(y, g, b):
    mu = jnp.mean(y, axis=-1, keepdims=True)
    d = y - mu
    var = jnp.mean(d * d, axis=-1, keepdims=True)
    return d * lax.rsqrt(var + LN_EPS) * g + b


def _mod_kernel(cv_ref, w_ref, b_ref, o_ref):
    a = jax.nn.silu(cv_ref[...]).astype(BF16)
    o_ref[...] = jnp.dot(a, w_ref[...].astype(BF16), preferred_element_type=F32) + b_ref[...]


def _modulation(cv, ada_w, ada_b):
    depth, d, nd = ada_w.shape
    tn = _pick_tile(nd, 1024)
    out = pl.pallas_call(
        _mod_kernel,
        grid=(depth, nd // tn),
        in_specs=[
            pl.BlockSpec((MOD_ROWS, d), lambda l, n: (0, 0)),
            pl.BlockSpec((None, d, tn), lambda l, n: (l, 0, n)),
            pl.BlockSpec((None, 1, tn), lambda l, n: (l, 0, n)),
        ],
        out_specs=pl.BlockSpec((None, MOD_ROWS, tn), lambda l, n: (l, 0, n)),
        out_shape=jax.ShapeDtypeStruct((depth, MOD_ROWS, nd), F32),
        compiler_params=_params("arbitrary", "arbitrary"),
        name="modulation",
    )(cv, ada_w, ada_b.reshape(depth, 1, nd))
    return out.reshape(depth, MOD_ROWS, N_MOD, d)


def _ffn_kernel(xa_hbm, xb_hbm, mod_ref, lng_ref, lnb_ref, wig_ref, wiu_ref, wo_ref, o_ref, x_buf, h_ref, x_sem,
                *, piece, alpha, nj, chunk, n_tiles, x_tile0, xa_tiles):
    i = pl.program_id(0)
    j = pl.program_id(1)
    tm, d = o_ref.shape
    gate = 0.5 * mod_ref[3 * piece + 2:3 * piece + 3, :]

    def x_copy(tile, act):
        def run(src, t):
            start = pl.multiple_of(t * tm, tm)
            act(pltpu.make_async_copy(src.at[pl.ds(start, tm), :], x_buf, x_sem))

        tile = tile + x_tile0
        pl.when(tile < xa_tiles)(lambda: run(xa_hbm, tile))
        pl.when(tile >= xa_tiles)(lambda: run(xb_hbm, tile - xa_tiles))

    def start(cp):
        cp.start()

    def wait(cp):
        cp.wait()

    def modulated(rows):
        sh = mod_ref[3 * piece:3 * piece + 1, :]
        sc = mod_ref[3 * piece + 1:3 * piece + 2, :]
        return (x_buf[rows, :] * (1.0 + sc) + sh).astype(BF16)

    def up(h):
        tf = wig_ref.shape[1]
        parts = []
        for c in range(0, tf, MXU_WIDTH):
            cols = slice(c, min(c + MXU_WIDTH, tf))
            g = jnp.dot(h, wig_ref[:, cols], preferred_element_type=F32)
            u = jnp.dot(h, wiu_ref[:, cols], preferred_element_type=F32)
            parts.append((jax.nn.silu(g) * u).astype(BF16))
        return jnp.concatenate(parts, axis=1)

    def gated_down(a, cols):
        return gate[:, cols] * jnp.dot(a, wo_ref[:, cols], preferred_element_type=F32)

    def layer_norm(y):
        return _layer_norm_rows(y, lng_ref[piece:piece + 1, :], lnb_ref[piece:piece + 1, :])

    col_slabs = _row_chunks(d, DOWN_SLAB)
    everything = slice(0, d)

    @pl.when(jnp.logical_and(i == 0, j == 0))
    def _():
        x_copy(i, start)

    @pl.when(j == 0)
    def _():
        x_copy(i, wait)

    if nj == 1:
        for rows in _row_chunks(tm, chunk):
            y = alpha * x_buf[rows, :] + gated_down(up(modulated(rows)), everything)
            o_ref[rows, :] = layer_norm(y)
    else:
        @pl.when(j == 0)
        def _():
            h = modulated(slice(0, tm))
            h_ref[...] = h
            a = up(h)
            for cols in col_slabs:
                o_ref[:, cols] = alpha * x_buf[:, cols] + gated_down(a, cols)

        @pl.when(jnp.logical_and(j > 0, j < nj - 1))
        def _():
            a = up(h_ref[...])
            for cols in col_slabs:
                o_ref[:, cols] += gated_down(a, cols)

        @pl.when(j == nj - 1)
        def _():
            for rows in _row_chunks(tm, chunk):
                y = o_ref[rows, :] + gated_down(up(h_ref[rows, :]), everything)
                o_ref[rows, :] = layer_norm(y)

    @pl.when(jnp.logical_and(j == 0, i + 1 < n_tiles))
    def _():
        x_copy(i + 1, start)


def _ffn(xs, mod, ln_g, ln_b, wi, wo, *, layer, piece, alpha, row_of_tile, tm, first_tile=0, n_tiles=None):
    xa, xb = xs if isinstance(xs, tuple) else (xs, xs)
    d = xa.shape[1]
    f = wo.shape[1]
    tf = _pick_tile(f, 512)
    nj = f // tf
    l = layer
    xa_tiles = xa.shape[0] // tm
    if n_tiles is None:
        n_tiles = xa_tiles + (xb.shape[0] // tm if isinstance(xs, tuple) else 0) - first_tile
    if not isinstance(xs, tuple):
        assert first_tile + n_tiles <= xa_tiles
    return pl.pallas_call(
        functools.partial(_ffn_kernel, piece=piece, alpha=alpha, nj=nj, chunk=_pick_tile(tm, 256),
                          n_tiles=n_tiles, x_tile0=first_tile, xa_tiles=xa_tiles),
        grid=(n_tiles, nj),
        in_specs=[
            pl.BlockSpec(memory_space=pl.ANY),
            pl.BlockSpec(memory_space=pl.ANY),
            pl.BlockSpec((None, None, N_MOD, d), lambda i, j: (l, row_of_tile(i + first_tile, tm), 0, 0)),
            pl.BlockSpec((None, 3, d), lambda i, j: (l, 0, 0)),
            pl.BlockSpec((None, 3, d), lambda i, j: (l, 0, 0)),
            pl.BlockSpec((None, d, tf), lambda i, j: (l, 0, j)),
            pl.BlockSpec((None, d, tf), lambda i, j: (l, 0, j + nj)),
            pl.BlockSpec((None, tf, d), lambda i, j: (l, j, 0)),
        ],
        out_specs=pl.BlockSpec((tm, d), lambda i, j: (i, 0)),
        out_shape=jax.ShapeDtypeStruct((n_tiles * tm, d), F32),
        scratch_shapes=[pltpu.VMEM((tm, d), F32), pltpu.VMEM((tm, d), BF16), pltpu.SemaphoreType.DMA(())],
        compiler_params=_params("arbitrary", "arbitrary"),
        name=f"ffn{piece}",
    )(xa, xb, mod, ln_g, ln_b, wi, wi, wo)


def _rope_slab(xs, cos, sin_signed):
    lane = lax.broadcasted_iota(jnp.int32, xs.shape, 1)
    partner = jnp.where(lane % 2 == 0, pltpu.roll(xs, LANES - 1, 1), pltpu.roll(xs, 1, 1))
    return xs * cos + partner * sin_signed


def _proj_kernel(x_ref, mod_ref, cos_ref, sin_ref, w_ref, gain_ref, ws_ref, sb_ref,
                 q_ref, kv_ref, kf_ref, vf_ref, sgu_ref, hp_ref, cb_ref):
    sh = mod_ref[3:4, :]
    sc = mod_ref[4:5, :]
    h = (x_ref[...] * (1.0 + sc) + sh).astype(BF16)
    cos = cos_ref[...]
    sin = sin_ref[...]

    q = jnp.dot(h, w_ref[:, :ATTN_WIDTH], preferred_element_type=F32)
    for s in range(ATTN_WIDTH // LANES):
        cols = slice(s * LANES, (s + 1) * LANES)
        q_ref[:, cols] = (_rope_slab(q[:, cols], cos, sin) * ATTN_SCALE).astype(BF16)

    kv = jnp.dot(h, w_ref[:, ATTN_WIDTH:ATTN_WIDTH + 2 * KV_WIDTH], preferred_element_type=F32)
    for s in range(KV_WIDTH // LANES):
        cols = slice(s * LANES, (s + 1) * LANES)
        k = _rope_slab(kv[:, cols], cos, sin)
        kf_ref[:, cols] = k
        kv_ref[:, cols] = k.astype(BF16)
    v = kv[:, KV_WIDTH:]
    vf_ref[...] = v
    kv_ref[:, KV_WIDTH:] = v.astype(BF16)

    base = ATTN_WIDTH + 2 * KV_WIDTH
    su = jnp.dot(h, w_ref[:, base:base + SGU_WIDTH], preferred_element_type=F32)
    sv = jnp.dot(h, w_ref[:, base + SGU_WIDTH:base + 2 * SGU_WIDTH], preferred_element_type=F32)
    conv_in = jnp.dot(h, w_ref[:, base + 2 * SGU_WIDTH:], preferred_element_type=F32)
    cb_ref[...] = conv_in[:, CONV_WIDTH:2 * CONV_WIDTH]
    hp_ref[...] = conv_in[:, 2 * CONV_WIDTH:] * conv_in[:, :CONV_WIDTH]

    u = jax.nn.gelu(su)
    v = jax.nn.gelu(sv)
    mu = jnp.mean(v, axis=-1, keepdims=True)
    dv = v - mu
    var = jnp.mean(dv * dv, axis=-1, keepdims=True)
    vn = (dv * lax.rsqrt(var + LN_EPS) * gain_ref[...]).astype(BF16)
    for rows in _row_chunks(x_ref.shape[0], CHUNK):
        for head in range(SGU_HEADS):
            cols = slice(head * SGU_HEAD_DIM, (head + 1) * SGU_HEAD_DIM)
            mixed = jnp.dot(ws_ref[head], vn[rows, cols], preferred_element_type=F32) + sb_ref[:, cols]
            sgu_ref[rows, cols] = (u[rows, cols] * mixed).astype(BF16)


def _proj(x, mod, cos_tab, sin_tab, w_in, gain, ws, sb_full, *, layer, row_of_tile, tm, n_ctx_tokens, lat_seq):
    t, d = x.shape
    l = layer
    n_ctx_tiles = n_ctx_tokens // tm
    tiles_per_seq = lat_seq // tm

    def rope_idx(i):
        return (jnp.where(i < n_ctx_tiles, 0, 1 + jnp.maximum(i - n_ctx_tiles, 0) % tiles_per_seq), 0)

    def rows(width):
        return pl.BlockSpec((tm, width), lambda i: (i, 0))

    return pl.pallas_call(
        _proj_kernel,
        grid=(t // tm,),
        in_specs=[
            rows(d),
            pl.BlockSpec((None, None, N_MOD, d), lambda i: (l, row_of_tile(i, tm), 0, 0)),
            pl.BlockSpec((tm, LANES), rope_idx),
            pl.BlockSpec((tm, LANES), rope_idx),
            pl.BlockSpec((None, d, IN_WIDTH), lambda i: (l, 0, 0), pipeline_mode=pl.Buffered(1)),
            pl.BlockSpec((None, 1, SGU_WIDTH), lambda i: (l, 0, 0)),
            pl.BlockSpec((None, SGU_HEADS, CHUNK, CHUNK), lambda i: (l, 0, 0, 0)),
            pl.BlockSpec((None, CHUNK, SGU_WIDTH), lambda i: (l, 0, 0)),
        ],
        out_specs=[rows(ATTN_WIDTH), rows(2 * KV_WIDTH), rows(KV_WIDTH), rows(KV_WIDTH), rows(SGU_WIDTH),
                   rows(CONV_WIDTH), rows(CONV_WIDTH)],
        out_shape=[
            jax.ShapeDtypeStruct((t, ATTN_WIDTH), BF16),
            jax.ShapeDtypeStruct((t, 2 * KV_WIDTH), BF16),
            jax.ShapeDtypeStruct((t, KV_WIDTH), F32),
            jax.ShapeDtypeStruct((t, KV_WIDTH), F32),
            jax.ShapeDtypeStruct((t, SGU_WIDTH), BF16),
            jax.ShapeDtypeStruct((t, CONV_WIDTH), F32),
            jax.ShapeDtypeStruct((t, CONV_WIDTH), F32),
        ],
        compiler_params=_params("arbitrary"),
        name="in_proj",
    )(x, mod, cos_tab, sin_tab, w_in, gain, ws, sb_full)


def _lane_replicate(x, g):
    kg = x[:, g * HEAD_DIM:(g + 1) * HEAD_DIM].astype(BF16)
    return jnp.concatenate([kg] * GROUP, axis=1)


def _attend_group(sink_ref, q_ref, o_ref, rows, g, k_rep, v_rep, valid, past):
    lane = lax.broadcasted_iota(jnp.int32, (BLOCK, GROUP_WIDTH), 1)
    head_of_lane = lane // HEAD_DIM
    contract_last = (((1,), (1,)), ((), ()))
    cols = slice(g * GROUP_WIDTH, (g + 1) * GROUP_WIDTH)
    qg = q_ref[rows, cols]
    q_stack = jnp.concatenate(
        [jnp.where(head_of_lane == h, qg, jnp.zeros_like(qg)) for h in range(GROUP)], axis=0)
    s_loc = lax.dot_general(q_stack, k_rep, contract_last, preferred_element_type=F32)
    if past is not None:
        s_past = lax.dot_general(q_stack, past[0][g], contract_last, preferred_element_type=F32)
    e_loc, e_past, inv = [], [], []
    for h in range(GROUP):
        hrows = slice(h * BLOCK, (h + 1) * BLOCK)
        sink = sink_ref[g * GROUP + h]
        sl = jnp.where(valid, s_loc[hrows], NEG_INF)
        m = jnp.maximum(jnp.max(sl, axis=-1, keepdims=True), sink)
        if past is not None:
            sp = s_past[hrows]
            m = jnp.maximum(m, jnp.max(sp, axis=-1, keepdims=True))
        el = jnp.exp(sl - m)
        denom = jnp.sum(el, axis=-1, keepdims=True) + jnp.exp(sink - m)
        e_loc.append(el.astype(BF16))
        if past is not None:
            ep = jnp.exp(sp - m)
            denom = denom + jnp.sum(ep, axis=-1, keepdims=True)
            e_past.append(ep.astype(BF16))
        inv.append(1.0 / denom)
    pv = jnp.dot(jnp.concatenate(e_loc, axis=0), v_rep, preferred_element_type=F32)
    if past is not None:
        pv = pv + jnp.dot(jnp.concatenate(e_past, axis=0), past[1][g], preferred_element_type=F32)
    out = jnp.zeros((BLOCK, GROUP_WIDTH), F32)
    for h in range(GROUP):
        hrows = slice(h * BLOCK, (h + 1) * BLOCK)
        out = jnp.where(head_of_lane == h, pv[hrows] * inv[h], out)
    o_ref[rows, cols] = out.astype(o_ref.dtype)


def _attn_kernel(sink_ref, q_ref, kvp_ref, kvm_ref, kvn_ref, ck_ref, cv_ref, o_ref,
                 kpast_ref, vpast_ref, *, n_ctx_pairs, ctx_pairs_per_seq, lat_pairs_per_seq):
    n = pl.program_id(0)
    is_ctx = n < n_ctx_pairs
    pos = jnp.where(is_ctx, n % ctx_pairs_per_seq, (n - n_ctx_pairs) % lat_pairs_per_seq)
    last = jnp.where(is_ctx, ctx_pairs_per_seq - 1, lat_pairs_per_seq - 1)

    qi = lax.broadcasted_iota(jnp.int32, (BLOCK, 3 * BLOCK), 0)
    ki = lax.broadcasted_iota(jnp.int32, (BLOCK, 3 * BLOCK), 1) - BLOCK
    k_lo = (jnp.where(pos > 0, -BLOCK, 0), -BLOCK)
    k_hi = (2 * BLOCK - 1, jnp.where(pos < last, 2 * BLOCK - 1, BLOCK - 1))
    in_seq = [(ki >= k_lo[half]) & (ki <= k_hi[half]) for half in range(2)]

    kv_all = jnp.concatenate([kvp_ref[...], kvm_ref[...], kvn_ref[...]], axis=0)
    k_all = kv_all[:, :KV_WIDTH]
    v_all = kv_all[:, KV_WIDTH:]

    def attend(valid, past):
        for g in range(N_KV_HEADS):
            k_rep = _lane_replicate(k_all, g)
            v_rep = _lane_replicate(v_all, g)
            for half in range(2):
                keys = slice(half * BLOCK, (half + 3) * BLOCK)
                _attend_group(sink_ref, q_ref, o_ref, slice(half * BLOCK, (half + 1) * BLOCK), g,
                              k_rep[keys], v_rep[keys], valid[half], past)

    @pl.when(is_ctx)
    def _():
        attend(in_seq, None)

    @pl.when(jnp.logical_and(jnp.logical_not(is_ctx), pos == 0))
    def _():
        for g in range(N_KV_HEADS):
            kpast_ref[g] = _lane_replicate(ck_ref[...], g)
            vpast_ref[g] = _lane_replicate(cv_ref[...], g)

    @pl.when(jnp.logical_not(is_ctx))
    def _():
        band = jnp.abs(qi - ki) <= BLOCK
        attend([in_seq[half] & band for half in range(2)], (kpast_ref, vpast_ref))


def _attention(q, kv, sink, cache_k, cache_v, *, layer, n_ctx_tokens, ctx_seq, lat_seq):
    t = q.shape[0]
    pair = 2 * BLOCK
    assert ctx_seq == pair and lat_seq % pair == 0
    n_ctx_pairs = n_ctx_tokens // pair
    cps = ctx_seq // pair
    lps = lat_seq // pair
    past = cache_k.shape[2]
    l = layer

    def seq_pos(n):
        return jnp.where(n < n_ctx_pairs, n % cps, (n - n_ctx_pairs) % lps)

    def prev_blk(n):
        return (2 * n - (seq_pos(n) > 0).astype(jnp.int32), 0)

    def next_blk(n):
        last = jnp.where(n < n_ctx_pairs, cps - 1, lps - 1)
        return (2 * n + 1 + (seq_pos(n) < last).astype(jnp.int32), 0)

    def past_idx(n):
        return (jnp.maximum(n - n_ctx_pairs, 0) // lps, l, 0, 0)

    same = lambda n: (n, 0)
    return pl.pallas_call(
        functools.partial(_attn_kernel, n_ctx_pairs=n_ctx_pairs, ctx_pairs_per_seq=cps, lat_pairs_per_seq=lps),
        grid=(t // pair,),
        in_specs=[
            pl.BlockSpec(memory_space=pltpu.SMEM),
            pl.BlockSpec((pair, ATTN_WIDTH), same),
            pl.BlockSpec((BLOCK, 2 * KV_WIDTH), prev_blk),
            pl.BlockSpec((pair, 2 * KV_WIDTH), same),
            pl.BlockSpec((BLOCK, 2 * KV_WIDTH), next_blk),
            pl.BlockSpec((None, None, past, KV_WIDTH), past_idx),
            pl.BlockSpec((None, None, past, KV_WIDTH), past_idx),
        ],
        out_specs=pl.BlockSpec((pair, ATTN_WIDTH), same),
        out_shape=jax.ShapeDtypeStruct((t, ATTN_WIDTH), BF16),
        scratch_shapes=[pltpu.VMEM((N_KV_HEADS, past, GROUP_WIDTH), BF16),
                        pltpu.VMEM((N_KV_HEADS, past, GROUP_WIDTH), BF16)],
        compiler_params=_params("arbitrary"),
        name="attention",
    )(sink, q, kv, kv, kv, cache_k, cache_v)


def _mix_kernel(x_ref, mod_ref, lng_ref, lnb_ref, attn_ref, sgu_ref, hp_ref, cb_ref, hpp_ref, hpn_ref,
                cw_ref, wo_ref, o_ref,
                *, alpha, tm, ctx_seq, n_ctx_tiles, lat_tiles_per_seq):
    i = pl.program_id(0)
    is_ctx = i < n_ctx_tiles
    ctx_tiles_per_seq = max(ctx_seq // tm, 1)
    pos = jnp.where(is_ctx, i % ctx_tiles_per_seq, (i - n_ctx_tiles) % lat_tiles_per_seq)
    last = jnp.where(is_ctx, ctx_tiles_per_seq - 1, lat_tiles_per_seq - 1)

    hp = hp_ref[...]
    hp_prev = jnp.where(pos > 0, hpp_ref[SUBLANES - 1:SUBLANES, :], 0.0)
    hp_next = jnp.where(pos < last, hpn_ref[0:1, :], 0.0)
    row = lax.broadcasted_iota(jnp.int32, hp.shape, 0)
    if ctx_seq < tm:
        row = jnp.where(is_ctx, row & (ctx_seq - 1), row)
    last_row = jnp.where(is_ctx, min(ctx_seq, tm) - 1, tm - 1)
    up = jnp.where(row == 0, hp_prev, pltpu.roll(hp, 1, 0))
    dn = jnp.where(row == last_row, hp_next, pltpu.roll(hp, tm - 1, 0))
    y = cw_ref[0:1, :] * up + cw_ref[1:2, :] * hp + cw_ref[2:3, :] * dn
    conv = (cb_ref[...] * y).astype(BF16)

    gate = mod_ref[5:6, :]
    g = lng_ref[1:2, :]
    b = lnb_ref[1:2, :]
    for rows in _row_chunks(tm, OUT_ROWS):
        mix = jnp.concatenate([attn_ref[rows, :], sgu_ref[rows, :], conv[rows, :]], axis=1)
        res = jnp.dot(mix, wo_ref[...], preferred_element_type=F32)
        o_ref[rows, :] = _layer_norm_rows(alpha * x_ref[rows, :] + gate * res, g, b)


def _mix(x, mod, ln_g, ln_b, attn, sgu, hp, cb, conv_w, w_out, *, layer, alpha, row_of_tile,
         n_ctx_tokens, ctx_seq, lat_seq):
    t, d = x.shape
    tm = _pick_tile(min(n_ctx_tokens, lat_seq), 512)
    l = layer
    halo_per_tile = tm // SUBLANES
    n_halo_blocks = t // SUBLANES
    assert lat_seq % tm == 0 and (ctx_seq % tm == 0 or (tm % ctx_seq == 0 and ctx_seq & (ctx_seq - 1) == 0))

    def rows(width):
        return pl.BlockSpec((tm, width), lambda i: (i, 0))

    halo_prev = pl.BlockSpec((SUBLANES, CONV_WIDTH), lambda i: (jnp.maximum(i * halo_per_tile - 1, 0), 0))
    halo_next = pl.BlockSpec((SUBLANES, CONV_WIDTH),
                             lambda i: (jnp.minimum((i + 1) * halo_per_tile, n_halo_blocks - 1), 0))

    return pl.pallas_call(
        functools.partial(_mix_kernel, alpha=alpha, tm=tm, ctx_seq=ctx_seq, n_ctx_tiles=n_ctx_tokens // tm,
                          lat_tiles_per_seq=lat_seq // tm),
        grid=(t // tm,),
        in_specs=[
            pl.BlockSpec((tm, d), lambda i: (i, 0)),
            pl.BlockSpec((None, None, N_MOD, d), lambda i: (l, row_of_tile(i, tm), 0, 0)),
            pl.BlockSpec((None, 3, d), lambda i: (l, 0, 0)),
            pl.BlockSpec((None, 3, d), lambda i: (l, 0, 0)),
            rows(ATTN_WIDTH), rows(SGU_WIDTH), rows(CONV_WIDTH), rows(CONV_WIDTH), halo_prev, halo_next,
            pl.BlockSpec((None, 3, CONV_WIDTH), lambda i: (l, 0, 0)),
            pl.BlockSpec((None, MIX_WIDTH, d), lambda i: (l, 0, 0), pipeline_mode=pl.Buffered(1)),
        ],
        out_specs=pl.BlockSpec((tm, d), lambda i: (i, 0)),
        out_shape=jax.ShapeDtypeStruct((t, d), F32),
        compiler_params=_params("arbitrary"),
        name="mix_out",
    )(x, mod, ln_g, ln_b, attn, sgu, hp, cb, hp, hp, conv_w, w_out)


def _rope_tables(lat_seq, identity_rows):
    pos = jnp.arange(lat_seq)
    row = (pos // GRID_W).astype(F32)
    col = (pos % GRID_W).astype(F32)
    n_freq = HEAD_DIM // 4
    inv = ROPE_BASE ** (-jnp.arange(n_freq, dtype=F32) / n_freq)
    ang = jnp.concatenate([row[:, None] * inv, col[:, None] * inv], axis=-1)
    cos = jnp.cos(ang)
    sin = jnp.sin(ang)
    cos_pairs = jnp.repeat(cos, 2, axis=-1)
    sin_pairs = jnp.stack([-sin, sin], axis=-1).reshape(lat_seq, HEAD_DIM)
    reps = LANES // HEAD_DIM
    cos_tab = jnp.concatenate([jnp.ones((identity_rows, LANES), F32), jnp.tile(cos_pairs, (1, reps))], axis=0)
    sin_tab = jnp.concatenate([jnp.zeros((identity_rows, LANES), F32), jnp.tile(sin_pairs, (1, reps))], axis=0)
    return cos_tab, sin_tab


def kernel(x_prompt, x_sample, cache_k, cache_v, c, c_ctx, ada_w, ada_b, ffn1_wi, ffn1_wo, ffn2_wi, ffn2_wo,
           w_in, w_out, attn_sink, sgu_gain, sgu_ws, sgu_b, conv_w, ln_g, ln_b):
    batch, ctx_seq, d = x_prompt.shape
    lat_batch, lat_seq, _ = x_sample.shape
    depth = ada_w.shape[0]
    past = cache_k.shape[2]
    n_ctx_tokens = batch * ctx_seq
    assert lat_batch + 1 <= MOD_ROWS
    assert ctx_seq % BLOCK == 0 and lat_seq % BLOCK == 0 and lat_seq % GRID_W == 0

    alpha = (2 * depth) ** 0.25

    def row_of_tile(i, tm):
        first_lat = n_ctx_tokens // tm
        return jnp.where(i < first_lat, 0, 1 + jnp.maximum(i - first_lat, 0) // (lat_seq // tm))

    cv = jnp.zeros((MOD_ROWS, d), F32).at[0].set(c_ctx).at[1:1 + lat_batch].set(c)
    mod = _modulation(cv, ada_w, ada_b)

    wi1, wo1 = ffn1_wi.astype(BF16), ffn1_wo.astype(BF16)
    wi2, wo2 = ffn2_wi.astype(BF16), ffn2_wo.astype(BF16)
    w_in_b, w_out_b = w_in.astype(BF16), w_out.astype(BF16)
    ws_b = sgu_ws.astype(BF16)
    sb_full = jnp.repeat(jnp.swapaxes(sgu_b, 1, 2), SGU_HEAD_DIM, axis=2)
    gain = sgu_gain.reshape(depth, 1, SGU_WIDTH)
    ck = cache_k.reshape(lat_batch, depth, past, KV_WIDTH)
    cvv = cache_v.reshape(lat_batch, depth, past, KV_WIDTH)

    tm = _pick_tile(min(n_ctx_tokens, lat_seq), 512)
    tm_ffn = _pick_tile(min(n_ctx_tokens, lat_seq), 1024)
    n_ctx_tiles = n_ctx_tokens // tm_ffn
    n_lat_tiles = lat_batch * lat_seq // tm_ffn
    cos_tab, sin_tab = _rope_tables(lat_seq, tm)
    x = (x_prompt.reshape(n_ctx_tokens, d), x_sample.reshape(lat_batch * lat_seq, d))
    common = dict(row_of_tile=row_of_tile)
    ffn_common = dict(alpha=alpha, tm=tm_ffn, **common)
    ks_new, vs_new = [], []
    for l in range(depth):
        x = _ffn(x, mod, ln_g, ln_b, wi1, wo1, layer=l, piece=0, **ffn_common)
        q, kv, kf, vf, sgu, hp, cb = _proj(x, mod, cos_tab, sin_tab, w_in_b, gain, ws_b, sb_full, layer=l, tm=tm,
                                           n_ctx_tokens=n_ctx_tokens, lat_seq=lat_seq, **common)
        attn = _attention(q, kv, attn_sink[l], ck, cvv, layer=l, n_ctx_tokens=n_ctx_tokens,
                          ctx_seq=ctx_seq, lat_seq=lat_seq)
        x = _mix(x, mod, ln_g, ln_b, attn, sgu, hp, cb, conv_w, w_out_b, layer=l, alpha=alpha,
                 n_ctx_tokens=n_ctx_tokens, ctx_seq=ctx_seq, lat_seq=lat_seq, **common)
        if l < depth - 1:
            x = _ffn(x, mod, ln_g, ln_b, wi2, wo2, layer=l, piece=2, **ffn_common)
        ks_new.append(kf[:n_ctx_tokens].reshape(batch, ctx_seq, N_KV_HEADS, HEAD_DIM))
        vs_new.append(vf[:n_ctx_tokens].reshape(batch, ctx_seq, N_KV_HEADS, HEAD_DIM))

    last = dict(layer=depth - 1, piece=2, **ffn_common)
    y_prompt = _ffn(x, mod, ln_g, ln_b, wi2, wo2, n_tiles=n_ctx_tiles, **last)
    y_sample = _ffn(x, mod, ln_g, ln_b, wi2, wo2, first_tile=n_ctx_tiles, n_tiles=n_lat_tiles, **last)
    y_prompt = y_prompt.reshape(batch, ctx_seq, d)
    y_sample = y_sample.reshape(lat_batch, lat_seq, d)
    return (y_prompt, y_sample, jnp.stack(ks_new, axis=1), jnp.stack(vs_new, axis=1))
```
